```python
import math
import jax, jax.numpy as jnp
from jax import lax
import numpy as np

D_MODEL = 1024
BATCH = 8
SEQ = 4096
DEPTH = 2

DN_HEADS = 8
DN_DK = 64
DN_DV = 64
DN_CONV = 5
DN_CHUNK = 64
ATTN_HEADS = 8
ATTN_KV_HEADS = 2
ATTN_GROUP = ATTN_HEADS // ATTN_KV_HEADS
ATTN_HD = 64
WINDOW = 128
ATTN_BLOCK = 128
ATTN_SPAN = ATTN_BLOCK + 2 * WINDOW
D_FF = 4 * D_MODEL
NORM_EPS = 1e-6

DN_QK = DN_HEADS * DN_DK
DN_V = DN_HEADS * DN_DV
DN_CONV_CH = 2 * DN_QK + DN_V
ATTN_Q = ATTN_HEADS * ATTN_HD
ATTN_KV = ATTN_KV_HEADS * ATTN_HD
IN_SPLITS = (DN_QK, DN_QK, DN_V, DN_V, 4 * DN_HEADS, ATTN_Q, ATTN_KV, ATTN_KV, 2 * D_MODEL)
IN_COLS = 2 * DN_QK + 2 * DN_V + 4 * DN_HEADS + ATTN_Q + 2 * ATTN_KV + 2 * D_MODEL

kernel_name = "hybrid_gdn_swa_bidir_encoder"


def rmsnorm(t, w):
    tf = t.astype(jnp.float32)
    y = tf * lax.rsqrt(jnp.mean(tf * tf, axis=-1, keepdims=True) + NORM_EPS)
    return (y * w.astype(jnp.float32)).astype(t.dtype)


def l2norm(t):
    tf = t.astype(jnp.float32)
    return tf * lax.rsqrt(jnp.sum(tf * tf, axis=-1, keepdims=True) + NORM_EPS)


def split_cols(h, sizes):
    out, o = [], 0
    for s in sizes:
        out.append(h[..., o:o + s])
        o += s
    return out


def short_conv(u, w):
    k = w.shape[0]
    return lax.conv_general_dilated(
        u, w[:, None, :].astype(u.dtype), window_strides=(1,),
        padding=[(k // 2, k // 2)], dimension_numbers=("NWC", "WIO", "NWC"),
        feature_group_count=u.shape[-1])


def chunk_gated_delta(q, k, v, g, beta):
    b_, s_, h_, dk = q.shape
    dv = v.shape[-1]
    n = s_ // DN_CHUNK
    f32 = jnp.float32

    def chunks(t):
        return jnp.moveaxis(t.astype(f32).reshape((b_, n, DN_CHUNK) + t.shape[2:]), 2, 3)

    q = chunks(q) * (dk ** -0.5)
    k = chunks(k)
    v = chunks(v)
    g = chunks(g)
    beta = chunks(beta)
    gc = jnp.cumsum(g, axis=-1)
    idx = jnp.arange(DN_CHUNK)
    incl = idx[:, None] >= idx[None, :]
    strict = idx[:, None] > idx[None, :]
    diff = gc[..., :, None] - gc[..., None, :]
    decay = jnp.exp(jnp.where(incl, diff, -jnp.inf))
    kb = k * beta[..., None]
    lmat = jnp.where(strict, jnp.einsum("bnhik,bnhjk->bnhij", kb, k) * decay, 0.0)
    rhs = jnp.concatenate([v * beta[..., None], kb * jnp.exp(gc)[..., None]], axis=-1)
    sol = lax.linalg.triangular_solve(lmat, rhs, left_side=True, lower=True, unit_diagonal=True)
    u = sol[..., :dv]
    w = sol[..., dv:]
    attn = jnp.einsum("bnhik,bnhjk->bnhij", q, k) * decay

    def step(state, inp):
        q_i, k_i, u_i, w_i, gc_i, a_i = inp
        v_new = u_i - jnp.einsum("bhck,bhkv->bhcv", w_i, state)
        o_i = (jnp.einsum("bhck,bhkv->bhcv", q_i * jnp.exp(gc_i)[..., None], state)
               + jnp.einsum("bhij,bhjv->bhiv", a_i, v_new))
        g_last = gc_i[..., -1]
        state = (state * jnp.exp(g_last)[..., None, None]
                 + jnp.einsum("bhck,bhcv->bhkv",
                              k_i * jnp.exp(g_last[..., None] - gc_i)[..., None], v_new))
        return state, o_i

    xs = tuple(jnp.moveaxis(t, 1, 0) for t in (q, k, u, w, gc, attn))
    state0 = jnp.zeros((b_, h_, dk, dv), f32)
    _, o = lax.scan(step, state0, xs)
    o = jnp.moveaxis(jnp.moveaxis(o, 0, 1), 2, 3)
    return o.reshape(b_, s_, h_, dv)


def window_attention(q, k, v, sink):
    b_, s_, _, hd = q.shape
    nb = s_ // ATTN_BLOCK
    qb = q.reshape(b_, nb, ATTN_BLOCK, ATTN_KV_HEADS, ATTN_GROUP, hd)
    pad = ((0, 0), (WINDOW, WINDOW), (0, 0), (0, 0))
    kp = jnp.pad(k, pad)
    vp = jnp.pad(v, pad)
    kidx = (jnp.arange(nb) * ATTN_BLOCK)[:, None] + jnp.arange(ATTN_SPAN)[None, :]
    kb = kp[:, kidx]
    vb = vp[:, kidx]
    scores = jnp.einsum("bnqhgd,bnkhd->bnhgqk", qb, kb).astype(jnp.float32) * (hd ** -0.5)
    tpos = (jnp.arange(nb) * ATTN_BLOCK)[:, None] + jnp.arange(ATTN_BLOCK)[None, :]
    spos = kidx - WINDOW
    dist = jnp.abs(tpos[:, :, None] - spos[:, None, :])
    valid = (dist <= WINDOW) & (spos[:, None, :] >= 0) & (spos[:, None, :] < s_)
    slopes = jnp.exp2(-8.0 * jnp.arange(1, ATTN_HEADS + 1, dtype=jnp.float32) / ATTN_HEADS)
    slopes = slopes.reshape(ATTN_KV_HEADS, ATTN_GROUP, 1, 1)
    bias = -slopes * dist[:, None, None].astype(jnp.float32)
    scores = jnp.where(valid[:, None, None], scores + bias, -jnp.inf)
    sink_l = jnp.broadcast_to(
        sink.astype(jnp.float32).reshape(ATTN_KV_HEADS, ATTN_GROUP, 1, 1),
        scores.shape[:-1] + (1,))
    probs = jax.nn.softmax(jnp.concatenate([scores, sink_l], axis=-1), axis=-1)[..., :-1]
    out = jnp.einsum("bnhgqk,bnkhd->bnqhgd", probs.astype(v.dtype), vb)
    return out.reshape(b_, s_, ATTN_HEADS * hd)


def setup_inputs(seed: int = 0) -> dict:
    key = jax.random.key(seed)
    ks = jax.random.split(key, 20)
    f = jnp.float32

    def nrm(k, shape, scale):
        return jax.random.normal(k, shape, f) * scale

    x = nrm(ks[0], (BATCH, SEQ, D_MODEL), 1.0)
    w_in = nrm(ks[1], (DEPTH, D_MODEL, IN_COLS), D_MODEL ** -0.5)
    conv_w = nrm(ks[2], (DEPTH, DN_CONV, DN_CONV_CH), DN_CONV ** -0.5)
    a_log = jnp.log(jax.random.uniform(ks[3], (DEPTH, 2, DN_HEADS), f, 0.5, 4.0))
    dt = jnp.exp(jax.random.uniform(ks[4], (DEPTH, 2, DN_HEADS), f,
                                    math.log(1e-3), math.log(1e-1)))
    dt_bias = dt + jnp.log(-jnp.expm1(-dt))
    dn_norm_w = 1.0 + nrm(ks[5], (DEPTH, DN_DV), 0.01)
    attn_sink = nrm(ks[6], (DEPTH, ATTN_HEADS), 0.5)
    w_up_a = nrm(ks[7], (DEPTH, DN_V, D_MODEL), DN_V ** -0.5)
    w_up_b = nrm(ks[8], (DEPTH, ATTN_Q, D_MODEL), ATTN_Q ** -0.5)
    w_out = nrm(ks[9], (DEPTH, D_MODEL, D_MODEL), D_MODEL ** -0.5)
    norm_mix_pre = 1.0 + nrm(ks[10], (DEPTH, D_MODEL), 0.01)
    norm_mix_post = 1.0 + nrm(ks[11], (DEPTH, D_MODEL), 0.01)
    norm_mlp_pre = 1.0 + nrm(ks[12], (DEPTH, D_MODEL), 0.01)
    norm_mlp_post = 1.0 + nrm(ks[13], (DEPTH, D_MODEL), 0.01)
    w_mlp_in = nrm(ks[14], (DEPTH, D_MODEL, D_FF), D_MODEL ** -0.5)
    w_mlp_out = nrm(ks[15], (DEPTH, D_FF, D_MODEL), D_FF ** -0.5)
    return {"x": x, "w_in": w_in, "conv_w": conv_w, "a_log": a_log, "dt_bias": dt_bias,
            "dn_norm_w": dn_norm_w, "attn_sink": attn_sink, "w_up_a": w_up_a,
            "w_up_b": w_up_b, "w_out": w_out, "norm_mix_pre": norm_mix_pre,
            "norm_mix_post": norm_mix_post, "norm_mlp_pre": norm_mlp_pre,
            "norm_mlp_post": norm_mlp_post, "w_mlp_in": w_mlp_in, "w_mlp_out": w_mlp_out}


def reference(x, w_in, conv_w, a_log, dt_bias, dn_norm_w, attn_sink, w_up_a, w_up_b, w_out,
              norm_mix_pre, norm_mix_post, norm_mlp_pre, norm_mlp_post, w_mlp_in, w_mlp_out):
    b_, s_, _ = x.shape
    flip = lambda t: jnp.flip(t, axis=1)
    for l in range(DEPTH):
        h = rmsnorm(x, norm_mix_pre[l])
        p = h @ w_in[l]
        dq, dk_, dv_, dz, dgates, aq, ak, av, bgates = split_cols(p, IN_SPLITS)

        qkv = jax.nn.silu(short_conv(jnp.concatenate([dq, dk_, dv_], axis=-1), conv_w[l]))
        dq, dk_, dv_ = split_cols(qkv, (DN_QK, DN_QK, DN_V))
        dq = l2norm(dq.reshape(b_, s_, DN_HEADS, DN_DK))
        dk_ = l2norm(dk_.reshape(b_, s_, DN_HEADS, DN_DK))
        dv_ = dv_.reshape(b_, s_, DN_HEADS, DN_DV).astype(jnp.float32)
        a_f, a_b, be_f, be_b = split_cols(dgates.astype(jnp.float32), (DN_HEADS,) * 4)
        decay_rate = jnp.exp(a_log[l].astype(jnp.float32))
        dtb = dt_bias[l].astype(jnp.float32)
        g_f = -decay_rate[0] * jax.nn.softplus(a_f + dtb[0])
        g_b = -decay_rate[1] * jax.nn.softplus(a_b + dtb[1])
        o_fwd = chunk_gated_delta(dq, dk_, dv_, g_f, jax.nn.sigmoid(be_f))
        o_bwd = flip(chunk_gated_delta(flip(dq), flip(dk_), flip(dv_), flip(g_b),
                                       flip(jax.nn.sigmoid(be_b))))
        o_a = (o_fwd + o_bwd).astype(x.dtype)
        o_a = rmsnorm(o_a, dn_norm_w[l]) * jax.nn.silu(dz.reshape(b_, s_, DN_HEADS, DN_DV))
        y_a = o_a.reshape(b_, s_, DN_V) @ w_up_a[l]

        o_b = window_attention(aq.reshape(b_, s_, ATTN_HEADS, ATTN_HD),
                               ak.reshape(b_, s_, ATTN_KV_HEADS, ATTN_HD),
                               av.reshape(b_, s_, ATTN_KV_HEADS, ATTN_HD), attn_sink[l])
        y_b = o_b @ w_up_b[l]

        g_ma, g_mb = split_cols(bgates, (D_MODEL, D_MODEL))
        mix = (jax.nn.sigmoid(g_ma) * y_a + jax.nn.sigmoid(g_mb) * y_b) @ w_out[l]
        x = x + rmsnorm(mix, norm_mix_post[l])

        h = rmsnorm(x, norm_mlp_pre[l])
        u = jnp.square(jax.nn.relu(h @ w_mlp_in[l]))
        x = x + rmsnorm(u @ w_mlp_out[l], norm_mlp_post[l])
    return x
```

```python
import functools

import jax
import jax.numpy as jnp
from jax import lax
from jax.experimental import pallas as pl
from jax.experimental.pallas import tpu as pltpu

F32 = jnp.float32
BF16 = jnp.bfloat16

NORM_EPS = 1e-6
DN_HEADS = 8
DN_DK = 64
DN_DV = 64
DN_CONV = 5
DN_CHUNK = 64
ATTN_HEADS = 8
ATTN_KV_HEADS = 2
ATTN_GROUP = ATTN_HEADS // ATTN_KV_HEADS
ATTN_HD = 64
WINDOW = 128
ATTN_BLOCK = 128

DN_QK = DN_HEADS * DN_DK
DN_V = DN_HEADS * DN_DV
DN_CONV_CH = 2 * DN_QK + DN_V
ATTN_Q = ATTN_HEADS * ATTN_HD
ATTN_KV = ATTN_KV_HEADS * ATTN_HD
N_GATES = 4 * DN_HEADS

LANES = 128
GATE_PAD = LANES
HEADS_PER_GROUP = 4
GROUP_W = HEADS_PER_GROUP * DN_DK
N_GROUPS = DN_HEADS // HEADS_PER_GROUP
HALO = 16

VMEM_LIMIT = 56 * 1024 * 1024


def _cparams(sem):
    return pltpu.CompilerParams(dimension_semantics=sem, vmem_limit_bytes=VMEM_LIMIT)


def _const_spec(shape):
    nd = len(shape)
    return pl.BlockSpec(shape, lambda *_: (0,) * nd)


def _dot(a, b):
    return jnp.dot(a, b, preferred_element_type=F32)


def _dot_nt(a, b):
    return lax.dot_general(a, b, (((1,), (1,)), ((), ())), preferred_element_type=F32)


def _dot_tn(a, b):
    return lax.dot_general(a, b, (((0,), (0,)), ((), ())), preferred_element_type=F32)


def _split_terms(x, terms):
    out, r = [], x
    for t in range(terms):
        hi = r.astype(BF16)
        out.append(hi)
        if t + 1 < terms:
            r = r - hi.astype(F32)
    return out


def _sel_dot(x, sel, terms):
    acc = None
    for piece in _split_terms(x, terms):
        d = _dot(piece, sel)
        acc = d if acc is None else acc + d
    return acc


def _sel_dot_left(sel, x, terms):
    acc = None
    for piece in _split_terms(x, terms):
        d = _dot(sel, piece)
        acc = d if acc is None else acc + d
    return acc


def _rms(x, w):
    ms = jnp.mean(x * x, axis=-1, keepdims=True)
    return x * lax.rsqrt(ms + NORM_EPS) * w


def _in_proj_kernel(x_ref, nw_ref, w_ref, wg_ref, p_ref, g_ref):
    h = _rms(x_ref[...], nw_ref[...]).astype(BF16)
    n = w_ref.shape[1]
    step = 512
    for c0 in range(0, n, step):
        c1 = min(c0 + step, n)
        p_ref[:, c0:c1] = _dot(h, w_ref[:, c0:c1]).astype(BF16)
    g_ref[...] = _dot(h, wg_ref[...])


def _in_proj(xf, nw, w, wg, tm):
    m, d = xf.shape
    n = w.shape[1]
    return pl.pallas_call(
        _in_proj_kernel,
        grid=(m // tm,),
        in_specs=[
            pl.BlockSpec((tm, d), lambda i: (i, 0)),
            _const_spec((1, d)),
            _const_spec((d, n)),
            _const_spec((d, GATE_PAD)),
        ],
        out_specs=[
            pl.BlockSpec((tm, n), lambda i: (i, 0)),
            pl.BlockSpec((tm, GATE_PAD), lambda i: (i, 0)),
        ],
        out_shape=[
            jax.ShapeDtypeStruct((m, n), BF16),
            jax.ShapeDtypeStruct((m, GATE_PAD), F32),
        ],
        compiler_params=_cparams(("parallel",)),
        name="in_proj",
    )(xf, nw, w, wg)


def _conv_kernel(cur_ref, prev_ref, next_ref, cw_ref, ones_ref, out_ref, ext_ref):
    j = pl.program_id(1)
    nj = pl.num_programs(1)
    ts = cur_ref.shape[1]
    pad = DN_CONV // 2
    ext_ref[0:HALO, :] = jnp.where(j > 0, prev_ref[0].astype(F32), 0.0)
    ext_ref[HALO:HALO + ts, :] = cur_ref[0].astype(F32)
    ext_ref[HALO + ts:2 * HALO + ts, :] = jnp.where(j < nj - 1, next_ref[0].astype(F32), 0.0)
    acc = None
    for k in range(DN_CONV):
        off = HALO - pad + k
        term = ext_ref[off:off + ts, :] * cw_ref[k:k + 1, :]
        acc = term if acc is None else acc + term
    y = acc * jax.nn.sigmoid(acc)
    ones = ones_ref[...]

    def l2n(t, scale):
        ss = _sel_dot(t * t, ones, 2)
        return t * (lax.rsqrt(ss + NORM_EPS) * scale)

    out_ref[0, :, 0:DN_QK] = l2n(y[:, 0:DN_QK], DN_DK ** -0.5).astype(BF16)
    out_ref[0, :, DN_QK:2 * DN_QK] = l2n(y[:, DN_QK:2 * DN_QK], 1.0).astype(BF16)
    out_ref[0, :, 2 * DN_QK:] = y[:, 2 * DN_QK:].astype(BF16)


def _conv(p3, cw, ones_hd, ts):
    b, s, _ = p3.shape
    nh = ts // HALO
    last = s // HALO - 1
    return pl.pallas_call(
        _conv_kernel,
        grid=(b, s // ts),
        in_specs=[
            pl.BlockSpec((1, ts, DN_CONV_CH), lambda i, j: (i, j, 0)),
            pl.BlockSpec((1, HALO, DN_CONV_CH), lambda i, j: (i, jnp.maximum(j * nh - 1, 0), 0)),
            pl.BlockSpec((1, HALO, DN_CONV_CH), lambda i, j: (i, jnp.minimum((j + 1) * nh, last), 0)),
            _const_spec((DN_CONV, DN_CONV_CH)),
            _const_spec((DN_QK, DN_QK)),
        ],
        out_specs=pl.BlockSpec((1, ts, DN_CONV_CH), lambda i, j: (i, j, 0)),
        out_shape=jax.ShapeDtypeStruct((b, s, DN_CONV_CH), BF16),
        scratch_shapes=[pltpu.VMEM((ts + 2 * HALO, DN_CONV_CH), F32)],
        compiler_params=_cparams(("parallel", "parallel")),
        name="short_conv",
    )(p3, p3, p3, cw, ones_hd)


def _softplus(x):
    return jnp.maximum(x, 0.0) + jnp.log1p(jnp.exp(-jnp.abs(x)))


def _block_diag(x, bd_mask):
    xb = x.astype(BF16)
    return jnp.concatenate([xb] * HEADS_PER_GROUP, axis=0) * bd_mask


def _delta_kernel(qkvf_ref, qkvb_ref, gf_ref, gb_ref, alog_ref, dtb_ref,
                  tri_ref, ones_ref, esel_ref, bd_ref,
                  of_ref, ob_ref,
                  gc_ref, eg_ref, ekg_ref, etot_ref, beta_ref, state_ref):
    j = pl.program_id(1)
    ts = qkvf_ref.shape[1]
    n_chunks = ts // DN_CHUNK
    c = DN_CHUNK

    @pl.when(j == 0)
    def _():
        state_ref[...] = jnp.zeros_like(state_ref)

    lane_g = lax.broadcasted_iota(jnp.int32, (1, GATE_PAD), 1)
    rate = jnp.where(lane_g < 2 * DN_HEADS, jnp.exp(alog_ref[...]), 0.0)
    dtb = dtb_ref[...]
    tri = tri_ref[...]
    ones = ones_ref[...]

    for d, g_ref in ((0, gf_ref), (1, gb_ref)):
        gates = g_ref[0]
        g = -rate * _softplus(gates + dtb)
        beta = jax.nn.sigmoid(gates)
        pre = _sel_dot_left(tri, g, 3)
        tot = _sel_dot_left(ones, g, 3)
        gcd = pre if d == 0 else tot - pre + g
        e_g = esel_ref[d]
        e_b = esel_ref[2 + d]
        gc_ref[d] = _sel_dot(gcd, e_g, 3)
        eg_ref[d] = _sel_dot(jnp.exp(gcd), e_g, 2)
        ekg_ref[d] = _sel_dot(jnp.exp(tot - gcd), e_g, 2)
        etot_ref[d] = _sel_dot(jnp.exp(tot), e_g, 2)
        beta_ref[d] = _sel_dot(beta, e_b, 2)

    bd_mask = bd_ref[...]
    bd_f32 = bd_mask.astype(F32)
    row = lax.broadcasted_iota(jnp.int32, (c, GROUP_W), 0)
    col = lax.broadcasted_iota(jnp.int32, (c, GROUP_W), 1) % c
    eye = row == col
    eye_f = eye.astype(F32)
    masks = ((row >= col, row > col), (row <= col, row < col))
    base = 8
    same_blk = lambda bs: (row // bs) == (col // bs)
    base_blk = same_blk(base)
    merge_blks = []
    bs = base
    while bs < c:
        merge_blks.append(jnp.logical_and(same_blk(2 * bs), jnp.logical_not(same_blk(bs))))
        bs *= 2

    def chain(qkv_ref, o_ref, r0, d, hg):
        rows = pl.ds(r0, c)
        l0 = hg * GROUP_W
        q = qkv_ref[0, rows, l0:l0 + GROUP_W]
        k = qkv_ref[0, rows, DN_QK + l0:DN_QK + l0 + GROUP_W]
        v = qkv_ref[0, rows, 2 * DN_QK + l0:2 * DN_QK + l0 + GROUP_W]
        qf, kf, vf = q.astype(F32), k.astype(F32), v.astype(F32)
        gcc = gc_ref[d, rows, l0:l0 + GROUP_W]
        egc = eg_ref[d, rows, l0:l0 + GROUP_W]
        ekg = ekg_ref[d, rows, l0:l0 + GROUP_W]
        beta = beta_ref[d, rows, l0:l0 + GROUP_W]
        etot = etot_ref[d, pl.ds(r0, 1), l0:l0 + GROUP_W]
        incl, strict = masks[d]

        bdk = jnp.concatenate([k] * HEADS_PER_GROUP, axis=0) * bd_mask
        r = _dot_nt(jnp.concatenate([q, k], axis=0), bdk)
        a_raw, kk = r[:c], r[c:]

        gcrow = jnp.sum(jnp.where(eye, gcc, 0.0), axis=0, keepdims=True)
        decay = jnp.where(incl, jnp.exp(jnp.where(incl, gcc - gcrow, 0.0)), 0.0)
        attn = a_raw * decay
        nmat = jnp.where(strict, -(kk * beta * decay), 0.0)

        n_base = jnp.where(base_blk, nmat, 0.0)
        p = eye_f + n_base
        n2 = _dot(n_base.astype(BF16), _block_diag(n_base, bd_mask))
        r = _dot(jnp.concatenate([p, n2], axis=0).astype(BF16), _block_diag(n2, bd_mask))
        p = p + r[:c]
        p = p + _dot(p.astype(BF16), _block_diag(r[c:], bd_mask))
        for off_blk in merge_blks:
            e = jnp.where(off_blk, nmat, 0.0)
            t = _dot(p.astype(BF16), _block_diag(e, bd_mask))
            p = p + _dot(t.astype(BF16), _block_diag(p, bd_mask))

        vb = vf * beta
        kbg = kf * (beta * egc)
        rhs = jnp.concatenate([_block_diag(vb, bd_mask), _block_diag(kbg, bd_mask)], axis=1)
        uw = _dot(p.astype(BF16), rhs)
        u, w = uw[:, :GROUP_W], uw[:, GROUP_W:]

        sidx = d * N_GROUPS + hg
        state = state_ref[sidx]
        qg = qf * egc
        r2 = _dot(jnp.concatenate([w, qg], axis=0).astype(BF16), state.astype(BF16))
        vnew = u - r2[:c]
        o = r2[c:] + _dot(attn.astype(BF16), _block_diag(vnew, bd_mask))
        o_ref[0, rows, l0:l0 + GROUP_W] = o.astype(o_ref.dtype)
        kg = kf * ekg
        upd = _dot_tn(kg.astype(BF16), vnew.astype(BF16))
        state_ref[sidx] = state * etot + upd * bd_f32

    def body(ci, carry):
        rf = pl.multiple_of(ci * c, c)
        rb = pl.multiple_of((n_chunks - 1 - ci) * c, c)
        for hg in range(N_GROUPS):
            chain(qkvf_ref, of_ref, rf, 0, hg)
            chain(qkvb_ref, ob_ref, rb, 1, hg)
        return carry

    lax.fori_loop(0, n_chunks, body, 0)


def _delta(qkvn, gates3, alog_row, dtb_row, consts, ts):
    b, s, _ = qkvn.shape
    nj = s // ts
    fwd = lambda i, j: (i, j, 0)
    bwd = lambda i, j: (i, nj - 1 - j, 0)
    exp_shape = (2, ts, DN_QK)
    return pl.pallas_call(
        _delta_kernel,
        grid=(b, nj),
        in_specs=[
            pl.BlockSpec((1, ts, DN_CONV_CH), fwd),
            pl.BlockSpec((1, ts, DN_CONV_CH), bwd),
            pl.BlockSpec((1, ts, GATE_PAD), fwd),
            pl.BlockSpec((1, ts, GATE_PAD), bwd),
            _const_spec((1, GATE_PAD)),
            _const_spec((1, GATE_PAD)),
            _const_spec((ts, ts)),
            _const_spec((ts, ts)),
            _const_spec((4, GATE_PAD, DN_QK)),
            _const_spec((GROUP_W, GROUP_W)),
        ],
        out_specs=[
            pl.BlockSpec((1, ts, DN_V), fwd),
            pl.BlockSpec((1, ts, DN_V), bwd),
        ],
        out_shape=[
            jax.ShapeDtypeStruct((b, s, DN_V), BF16),
            jax.ShapeDtypeStruct((b, s, DN_V), BF16),
        ],
        scratch_shapes=[
            pltpu.VMEM(exp_shape, F32),
            pltpu.VMEM(exp_shape, F32),
            pltpu.VMEM(exp_shape, F32),
            pltpu.VMEM(exp_shape, F32),
            pltpu.VMEM(exp_shape, F32),
            pltpu.VMEM((2 * N_GROUPS, GROUP_W, GROUP_W), F32),
        ],
        compiler_params=_cparams(("parallel", "arbitrary")),
        name="gated_delta",
    )(qkvn, qkvn, gates3, gates3, alog_row, dtb_row,
      consts["tri"], consts["ones_chunk"], consts["esel"], consts["bd"])


def _attn_kernel(q_ref, kvc_ref, kvp_ref, kvn_ref, sink_ref, o_ref, kv_ref):
    j = pl.program_id(1)
    nj = pl.num_programs(1)
    tq = q_ref.shape[1]
    blk = ATTN_BLOCK
    span = blk + 2 * WINDOW
    n_sub = tq // blk
    kv_ref[0:WINDOW] = kvp_ref[0]
    kv_ref[WINDOW:WINDOW + tq] = kvc_ref[0]
    kv_ref[WINDOW + tq:] = kvn_ref[0]

    lane = lax.broadcasted_iota(jnp.int32, (1, LANES), 1)
    lo = lane < ATTN_HD
    qi = lax.broadcasted_iota(jnp.int32, (blk, span), 0)
    kr = lax.broadcasted_iota(jnp.int32, (blk, span), 1)
    dist_i = jnp.abs(qi + WINDOW - kr)
    dist = dist_i.astype(F32)
    band = dist_i <= WINDOW
    zero = jnp.zeros((), BF16)

    for sb in range(n_sub):
        kwin = kv_ref[sb * blk:sb * blk + span, 0:ATTN_KV]
        vwin = kv_ref[sb * blk:sb * blk + span, ATTN_KV:2 * ATTN_KV]
        valid = band
        if sb == 0:
            valid = jnp.logical_and(valid, jnp.logical_or(j > 0, kr >= WINDOW))
        if sb == n_sub - 1:
            valid = jnp.logical_and(valid, jnp.logical_or(j < nj - 1, kr < WINDOW + blk))
        qs = []
        for g in range(ATTN_GROUP):
            qg = q_ref[0, sb * blk:(sb + 1) * blk, g * LANES:(g + 1) * LANES]
            qs.append(jnp.where(lo, qg, zero))
            qs.append(jnp.where(lo, zero, qg))
        s_all = _dot_nt(jnp.concatenate(qs, axis=0), kwin)
        probs = []
        for idx in range(2 * ATTN_GROUP):
            g, half = idx // 2, idx % 2
            head = g + ATTN_GROUP * half
            slope = 2.0 ** (-8.0 * (head + 1) / ATTN_HEADS)
            sink = sink_ref[head:head + 1, 0:1]
            sc = s_all[idx * blk:(idx + 1) * blk] * (ATTN_HD ** -0.5) - slope * dist
            sc = jnp.where(valid, sc, -jnp.inf)
            m = jnp.maximum(jnp.max(sc, axis=-1, keepdims=True), sink)
            e = jnp.exp(sc - m)
            den = jnp.sum(e, axis=-1, keepdims=True) + jnp.exp(sink - m)
            probs.append((e * (1.0 / den)).astype(BF16))
        p_all = jnp.concatenate(
            [jnp.concatenate([probs[2 * g], probs[2 * g + 1]], axis=1) for g in range(ATTN_GROUP)],
            axis=0)
        v_stack = jnp.concatenate([jnp.where(lo, vwin, zero), jnp.where(lo, zero, vwin)], axis=0)
        out = _dot(p_all, v_stack)
        for g in range(ATTN_GROUP):
            o_ref[0, sb * blk:(sb + 1) * blk, g * LANES:(g + 1) * LANES] = (
                out[g * blk:(g + 1) * blk].astype(BF16))


def _attn(p3, sink_b, q_col, kv_col, tq):
    b, s, _ = p3.shape
    nh = tq // WINDOW
    last = s // WINDOW - 1
    kvw = 2 * ATTN_KV
    return pl.pallas_call(
        _attn_kernel,
        grid=(b, s // tq),
        in_specs=[
            pl.BlockSpec((1, tq, ATTN_Q), lambda i, j: (i, j, q_col // ATTN_Q)),
            pl.BlockSpec((1, tq, kvw), lambda i, j: (i, j, kv_col // kvw)),
            pl.BlockSpec((1, WINDOW, kvw), lambda i, j: (i, jnp.maximum(j * nh - 1, 0), kv_col // kvw)),
            pl.BlockSpec((1, WINDOW, kvw), lambda i, j: (i, jnp.minimum((j + 1) * nh, last), kv_col // kvw)),
            _const_spec((ATTN_HEADS, LANES)),
        ],
        out_specs=pl.BlockSpec((1, tq, ATTN_Q), lambda i, j: (i, j, 0)),
        out_shape=jax.ShapeDtypeStruct((b, s, ATTN_Q), BF16),
        scratch_shapes=[pltpu.VMEM((tq + 2 * WINDOW, kvw), BF16)],
        compiler_params=_cparams(("parallel", "parallel")),
        name="window_attn",
    )(p3, p3, p3, p3, sink_b)


def _merge_kernel(of_ref, ob_ref, z_ref, oatt_ref, bg_ref, x_ref,
                  dnw_ref, ones_ref, wa_ref, wb_ref, wo_ref, nw_ref, out_ref):
    d_model = x_ref.shape[1]
    oa = of_ref[...].astype(F32) + ob_ref[...].astype(F32)
    ms = _sel_dot(oa * oa, ones_ref[...], 2) * (1.0 / DN_DV)
    z = z_ref[...].astype(F32)
    on = oa * lax.rsqrt(ms + NORM_EPS) * dnw_ref[...] * (z * jax.nn.sigmoid(z))
    ya = _dot(on.astype(BF16), wa_ref[...])
    yb = _dot(oatt_ref[...], wb_ref[...])
    ga = jax.nn.sigmoid(bg_ref[:, 0:d_model].astype(F32))
    gb = jax.nn.sigmoid(bg_ref[:, d_model:2 * d_model].astype(F32))
    mix = _dot((ga * ya + gb * yb).astype(BF16), wo_ref[...])
    out_ref[...] = x_ref[...] + _rms(mix, nw_ref[...])


def _merge(o_f, o_b, p, o_att, xf, dnw, ones_hd, wa, wb, wo, nw, cols, tm):
    m, d = xf.shape
    row = lambda i: (i, 0)
    return pl.pallas_call(
        _merge_kernel,
        grid=(m // tm,),
        in_specs=[
            pl.BlockSpec((tm, DN_V), row),
            pl.BlockSpec((tm, DN_V), row),
            pl.BlockSpec((tm, DN_V), lambda i: (i, cols["z"] // DN_V)),
            pl.BlockSpec((tm, ATTN_Q), row),
            pl.BlockSpec((tm, 2 * d), lambda i: (i, cols["bg"] // (2 * d))),
            pl.BlockSpec((tm, d), row),
            _const_spec((1, DN_V)),
            _const_spec((DN_V, DN_V)),
            _const_spec((DN_V, d)),
            _const_spec((ATTN_Q, d)),
            _const_spec((d, d)),
            _const_spec((1, d)),
        ],
        out_specs=pl.BlockSpec((tm, d), row),
        out_shape=jax.ShapeDtypeStruct((m, d), F32),
        compiler_params=_cparams(("parallel",)),
        name="merge_out",
    )(o_f, o_b, p, o_att, p, xf, dnw, ones_hd, wa, wb, wo, nw)


def _mlp_kernel(x_ref, npre_ref, w1_ref, w2_ref, npost_ref, out_ref):
    x = x_ref[...]
    h = _rms(x, npre_ref[...]).astype(BF16)
    d_ff = w1_ref.shape[1]
    step = 1024
    acc = None
    for c0 in range(0, d_ff, step):
        u = jnp.maximum(_dot(h, w1_ref[:, c0:c0 + step]), 0.0)
        t = _dot((u * u).astype(BF16), w2_ref[c0:c0 + step, :])
        acc = t if acc is None else acc + t
    out_ref[...] = x + _rms(acc, npost_ref[...])


def _mlp(xf, npre, w1, w2, npost, tm):
    m, d = xf.shape
    d_ff = w1.shape[1]
    row = lambda i: (i, 0)
    single = pl.Buffered(1)
    return pl.pallas_call(
        _mlp_kernel,
        grid=(m // tm,),
        in_specs=[
            pl.BlockSpec((tm, d), row),
            _const_spec((1, d)),
            pl.BlockSpec((d, d_ff), lambda i: (0, 0), pipeline_mode=single),
            pl.BlockSpec((d_ff, d), lambda i: (0, 0), pipeline_mode=single),
            _const_spec((1, d)),
        ],
        out_specs=pl.BlockSpec((tm, d), row),
        out_shape=jax.ShapeDtypeStruct((m, d), F32),
        compiler_params=_cparams(("parallel",)),
        name="mlp",
    )(xf, npre, w1, w2, npost)


def _constants(ts):
    idx = jnp.arange(ts)
    same_chunk = (idx[:, None] // DN_CHUNK) == (idx[None, :] // DN_CHUNK)
    tri = jnp.logical_and(same_chunk, idx[:, None] >= idx[None, :]).astype(BF16)
    ones_chunk = same_chunk.astype(BF16)
    lanes = jnp.arange(DN_QK) // DN_DK
    gate_rows = jnp.arange(GATE_PAD)
    esel = jnp.stack([(gate_rows[:, None] == (off + lanes)[None, :]).astype(BF16)
                      for off in (0, DN_HEADS, 2 * DN_HEADS, 3 * DN_HEADS)])
    gi = jnp.arange(GROUP_W) // DN_DK
    bd = (gi[:, None] == gi[None, :]).astype(BF16)
    hi = jnp.arange(DN_QK) // DN_DK
    ones_hd = (hi[:, None] == hi[None, :]).astype(BF16)
    return {"tri": tri, "ones_chunk": ones_chunk, "esel": esel, "bd": bd, "ones_hd": ones_hd}


def _pick(total, want):
    t = min(want, total)
    while total % t:
        t //= 2
    return t


def kernel(x, w_in, conv_w, a_log, dt_bias, dn_norm_w, attn_sink, w_up_a, w_up_b, w_out,
           norm_mix_pre, norm_mix_post, norm_mlp_pre, norm_mlp_post, w_mlp_in, w_mlp_out):
    b, s, d = x.shape
    depth = w_in.shape[0]
    m = b * s
    tm = _pick(m, 512)
    ts_conv = _pick(s, 512)
    ts_delta = _pick(s, 256)
    tq = _pick(s, 512)
    consts = _constants(ts_delta)

    o_dz = DN_CONV_CH
    o_g = o_dz + DN_V
    o_aq = o_g + N_GATES
    o_ak = o_aq + ATTN_Q
    o_bg = o_ak + 2 * ATTN_KV
    cols = {"qkv": 0, "z": DN_CONV_CH, "bg": 2 * d, "aq": 4 * d, "akv": 4 * d + ATTN_Q}
    head_order = [h for g in range(ATTN_GROUP) for h in (g, g + ATTN_GROUP)]
    aq_perm = jnp.concatenate([jnp.arange(h * ATTN_HD, (h + 1) * ATTN_HD) for h in head_order])

    xf = x.reshape(m, d)
    for l in range(depth):
        wl = w_in[l]
        aq_cols = wl[:, o_aq:o_aq + ATTN_Q][:, aq_perm]
        w_main = jnp.concatenate(
            [wl[:, 0:o_g], wl[:, o_bg:o_bg + 2 * d], aq_cols, wl[:, o_ak:o_ak + 2 * ATTN_KV]],
            axis=1).astype(BF16)
        w_gate = jnp.pad(wl[:, o_g:o_g + N_GATES], ((0, 0), (0, GATE_PAD - N_GATES))).astype(BF16)
        row = lambda v: v.reshape(1, -1).astype(F32)

        p, gates = _in_proj(xf, row(norm_mix_pre[l]), w_main, w_gate, tm)
        p3 = p.reshape(b, s, p.shape[1])
        qkvn = _conv(p3, conv_w[l].astype(F32), consts["ones_hd"], ts_conv)

        alog_row = jnp.pad(a_log[l].reshape(1, -1).astype(F32), ((0, 0), (0, GATE_PAD - 2 * DN_HEADS)))
        dtb_row = jnp.pad(dt_bias[l].reshape(1, -1).astype(F32), ((0, 0), (0, GATE_PAD - 2 * DN_HEADS)))
        o_f, o_b = _delta(qkvn, gates.reshape(b, s, GATE_PAD), alog_row, dtb_row, consts, ts_delta)

        sink_b = jnp.broadcast_to(attn_sink[l].astype(F32)[:, None], (ATTN_HEADS, LANES))
        o_att = _attn(p3, sink_b, cols["aq"], cols["akv"], tq)

        dnw = jnp.tile(dn_norm_w[l].astype(F32), DN_HEADS).reshape(1, DN_V)
        xf = _merge(o_f.reshape(m, DN_V), o_b.reshape(m, DN_V), p, o_att.reshape(m, ATTN_Q), xf,
                    dnw, consts["ones_hd"], w_up_a[l].astype(BF16),
                    w_up_b[l][aq_perm, :].astype(BF16), w_out[l].astype(BF16),
                    row(norm_mix_post[l]), cols, tm)
        xf = _mlp(xf, row(norm_mlp_pre[l]), w_mlp_in[l].astype(BF16), w_mlp_out[l].astype(BF16),
                  row(norm_mlp_post[l]), tm)
    return xf.reshape(b, s, d)
```

```python
import functools

import jax
import jax.numpy as jnp
from jax import lax
from jax.experimental import pallas as pl
from jax.experimental.pallas import tpu as pltpu

F32 = jnp.float32
BF16 = jnp.bfloat16

NORM_EPS = 1e-6
DN_HEADS = 8
DN_DK = 64
DN_DV = 64
DN_CONV = 5
DN_CHUNK = 64
ATTN_HEADS = 8
ATTN_KV_HEADS = 2
ATTN_GROUP = ATTN_HEADS // ATTN_KV_HEADS
ATTN_HD = 64
WINDOW = 128
ATTN_BLOCK = 128

DN_QK = DN_HEADS * DN_DK
DN_V = DN_HEADS * DN_DV
DN_CONV_CH = 2 * DN_QK + DN_V
ATTN_Q = ATTN_HEADS * ATTN_HD
ATTN_KV = ATTN_KV_HEADS * ATTN_HD
N_GATES = 4 * DN_HEADS

LANES = 128
GATE_PAD = LANES
HEADS_PER_GROUP = 4
GROUP_W = HEADS_PER_GROUP * DN_DK
N_GROUPS = DN_HEADS // HEADS_PER_GROUP
HALO = 16

VMEM_LIMIT = 56 * 1024 * 1024


def _cparams(sem):
    return pltpu.CompilerParams(dimension_semantics=sem, vmem_limit_bytes=VMEM_LIMIT)


def _const_spec(shape):
    nd = len(shape)
    return pl.BlockSpec(shape, lambda *_: (0,) * nd)


def _dot(a, b):
    return jnp.dot(a, b, preferred_element_type=F32)


def _dot_nt(a, b):
    return lax.dot_general(a, b, (((1,), (1,)), ((), ())), preferred_element_type=F32)


def _dot_tn(a, b):
    return lax.dot_general(a, b, (((0,), (0,)), ((), ())), preferred_element_type=F32)


def _split_terms(x, terms):
    out, r = [], x
    for t in range(terms):
        hi = r.astype(BF16)
        out.append(hi)
        if t + 1 < terms:
            r = r - hi.astype(F32)
    return out


def _sel_dot(x, sel, terms):
    acc = None
    for piece in _split_terms(x, terms):
        d = _dot(piece, sel)
        acc = d if acc is None else acc + d
    return acc


def _sel_dot_left(sel, x, terms):
    acc = None
    for piece in _split_terms(x, terms):
        d = _dot(sel, piece)
        acc = d if acc is None else acc + d
    return acc


def _rms(x, w):
    ms = jnp.mean(x * x, axis=-1, keepdims=True)
    return x * lax.rsqrt(ms + NORM_EPS) * w


def _in_proj_kernel(x_ref, nw_ref, w_ref, wg_ref, p_ref, g_ref):
    h = _rms(x_ref[...], nw_ref[...]).astype(BF16)
    n = w_ref.shape[1]
    step = 512
    for c0 in range(0, n, step):
        c1 = min(c0 + step, n)
        p_ref[:, c0:c1] = _dot(h, w_ref[:, c0:c1]).astype(BF16)
    g_ref[...] = _dot(h, wg_ref[...])


def _in_proj(xf, nw, w, wg, tm):
    m, d = xf.shape
    n = w.shape[1]
    return pl.pallas_call(
        _in_proj_kernel,
        grid=(m // tm,),
        in_specs=[
            pl.BlockSpec((tm, d), lambda i: (i, 0)),
            _const_spec((1, d)),
            _const_spec((d, n)),
            _const_spec((d, GATE_PAD)),
        ],
        out_specs=[
            pl.BlockSpec((tm, n), lambda i: (i, 0)),
            pl.BlockSpec((tm, GATE_PAD), lambda i: (i, 0)),
        ],
        out_shape=[
            jax.ShapeDtypeStruct((m, n), BF16),
            jax.ShapeDtypeStruct((m, GATE_PAD), F32),
        ],
        compiler_params=_cparams(("parallel",)),
        name="in_proj",
    )(xf, nw, w, wg)


def _conv_kernel(cur_ref, prev_ref, next_ref, cw_ref, ones_ref, out_ref, ext_ref):
    j = pl.program_id(1)
    nj = pl.num_programs(1)
    ts = cur_ref.shape[1]
    pad = DN_CONV // 2
    ext_ref[0:HALO, :] = jnp.where(j > 0, prev_ref[0].astype(F32), 0.0)
    ext_ref[HALO:HALO + ts, :] = cur_ref[0].astype(F32)
    ext_ref[HALO + ts:2 * HALO + ts, :] = jnp.where(j < nj - 1, next_ref[0].astype(F32), 0.0)
    acc = None
    for k in range(DN_CONV):
        off = HALO - pad + k
        term = ext_ref[off:off + ts, :] * cw_ref[k:k + 1, :]
        acc = term if acc is None else acc + term
    y = acc * jax.nn.sigmoid(acc)
    ones = ones_ref[...]

    def l2n(t, scale):
        ss = _sel_dot(t * t, ones, 2)
        return t * (lax.rsqrt(ss + NORM_EPS) * scale)

    out_ref[0, :, 0:DN_QK] = l2n(y[:, 0:DN_QK], DN_DK ** -0.5).astype(BF16)
    out_ref[0, :, DN_QK:2 * DN_QK] = l2n(y[:, DN_QK:2 * DN_QK], 1.0).astype(BF16)
    out_ref[0, :, 2 * DN_QK:] = y[:, 2 * DN_QK:].astype(BF16)


def _conv(p3, cw, ones_hd, ts):
    b, s, _ = p3.shape
    nh = ts // HALO
    last = s // HALO - 1
    return pl.pallas_call(
        _conv_kernel,
        grid=(b, s // ts),
        in_specs=[
            pl.BlockSpec((1, ts, DN_CONV_CH), lambda i, j: (i, j, 0)),
            pl.BlockSpec((1, HALO, DN_CONV_CH), lambda i, j: (i, jnp.maximum(j * nh - 1, 0), 0)),
            pl.BlockSpec((1, HALO, DN_CONV_CH), lambda i, j: (i, jnp.minimum((j + 1) * nh, last), 0)),
            _const_spec((DN_CONV, DN_CONV_CH)),
            _const_spec((DN_QK, DN_QK)),
        ],
        out_specs=pl.BlockSpec((1, ts, DN_CONV_CH), lambda i, j: (i, j, 0)),
        out_shape=jax.ShapeDtypeStruct((b, s, DN_CONV_CH), BF16),
        scratch_shapes=[pltpu.VMEM((ts + 2 * HALO, DN_CONV_CH), F32)],
        compiler_params=_cparams(("parallel", "parallel")),
        name="short_conv",
    )(p3, p3, p3, cw, ones_hd)


def _softplus(x):
    return jnp.maximum(x, 0.0) + jnp.log1p(jnp.exp(-jnp.abs(x)))


def _block_diag(x, bd_mask):
    xb = x.astype(BF16)
    return jnp.concatenate([xb] * HEADS_PER_GROUP, axis=0) * bd_mask


def _round_robin(gens):
    live = list(gens)
    while live:
        nxt = []
        for g in live:
            try:
                next(g)
                nxt.append(g)
            except StopIteration:
                pass
        live = nxt


def _delta_kernel(qkvf_ref, qkvb_ref, gf_ref, gb_ref, alog_ref, dtb_ref,
                  tri_ref, esel_ref, bd_ref,
                  of_ref, ob_ref,
                  gc_ref, beta_ref, u_ref, wq_ref, attn_ref, kg_ref, state_ref):
    j = pl.program_id(1)
    ts = qkvf_ref.shape[1]
    n_chunks = ts // DN_CHUNK
    c = DN_CHUNK

    @pl.when(j == 0)
    def _():
        state_ref[...] = jnp.zeros_like(state_ref)

    lane_g = lax.broadcasted_iota(jnp.int32, (1, GATE_PAD), 1)
    rate = jnp.where(lane_g < 2 * DN_HEADS, jnp.exp(alog_ref[...]), 0.0)
    dtb = dtb_ref[...]

    for d, g_ref in ((0, gf_ref), (1, gb_ref)):
        gates = g_ref[0]
        g = -rate * _softplus(gates + dtb)
        gcd = _sel_dot_left(tri_ref[d], g, 3)
        gc_ref[d] = _sel_dot(gcd, esel_ref[d], 3)
        beta_ref[d] = _sel_dot(jax.nn.sigmoid(gates), esel_ref[2 + d], 2)

    bd_mask = bd_ref[...]
    bd_f32 = bd_mask.astype(F32)
    row = lax.broadcasted_iota(jnp.int32, (c, GROUP_W), 0)
    col = lax.broadcasted_iota(jnp.int32, (c, GROUP_W), 1) % c
    eye = row == col
    eye_f = eye.astype(F32)
    masks = ((row >= col, row > col), (row <= col, row < col))
    base = 8
    same_blk = lambda bs: (row // bs) == (col // bs)
    base_blk = same_blk(base)
    merge_blks = []
    bs = base
    while bs < c:
        merge_blks.append(jnp.logical_and(same_blk(2 * bs), jnp.logical_not(same_blk(bs))))
        bs *= 2

    def slot_of(d, hg, chunk):
        return (d * N_GROUPS + hg) * n_chunks + chunk

    def prep(qkv_ref, chunk, d, hg):
        rows = pl.ds(pl.multiple_of(chunk * c, c), c)
        l0 = hg * GROUP_W
        q = qkv_ref[0, rows, l0:l0 + GROUP_W]
        k = qkv_ref[0, rows, DN_QK + l0:DN_QK + l0 + GROUP_W]
        v = qkv_ref[0, rows, 2 * DN_QK + l0:2 * DN_QK + l0 + GROUP_W]
        incl, strict = masks[d]

        bdk = jnp.concatenate([k] * HEADS_PER_GROUP, axis=0) * bd_mask
        r = _dot_nt(jnp.concatenate([q, k], axis=0), bdk)
        yield
        gcc = gc_ref[d, rows, l0:l0 + GROUP_W]
        beta = beta_ref[d, rows, l0:l0 + GROUP_W]
        gcrow = jnp.sum(jnp.where(eye, gcc, 0.0), axis=0, keepdims=True)
        decay = jnp.where(incl, jnp.exp(jnp.where(incl, gcc - gcrow, 0.0)), 0.0)
        attn_ref[slot_of(d, hg, chunk)] = (r[:c] * decay).astype(BF16)
        nmat = jnp.where(strict, -(r[c:] * beta * decay), 0.0)

        n_base = jnp.where(base_blk, nmat, 0.0)
        p = eye_f + n_base
        n2 = _dot(n_base.astype(BF16), _block_diag(n_base, bd_mask))
        yield
        r = _dot(jnp.concatenate([p, n2], axis=0).astype(BF16), _block_diag(n2, bd_mask))
        yield
        p = p + r[:c]
        x = _dot(p.astype(BF16), _block_diag(r[c:], bd_mask))
        yield
        p = p + x
        for off_blk in merge_blks:
            e = jnp.where(off_blk, nmat, 0.0)
            t = _dot(p.astype(BF16), _block_diag(e, bd_mask))
            yield
            x = _dot(t.astype(BF16), _block_diag(p, bd_mask))
            yield
            p = p + x

        kf, vf = k.astype(F32), v.astype(F32)
        egc = jnp.exp(gcc)
        g_end = gcc[c - 1:c, :] if d == 0 else gcc[0:1, :]
        rhs = jnp.concatenate([_block_diag(vf * beta, bd_mask),
                               _block_diag(kf * (beta * egc), bd_mask)], axis=1)
        uw = _dot(p.astype(BF16), rhs)
        yield
        slot = slot_of(d, hg, chunk)
        u_ref[slot] = uw[:, :GROUP_W]
        wq_ref[slot, 0:c, :] = uw[:, GROUP_W:].astype(BF16)
        wq_ref[slot, c:2 * c, :] = (q.astype(F32) * egc).astype(BF16)
        kg_ref[slot] = (kf * jnp.exp(g_end - gcc)).astype(BF16)

    def scan(o_ref, chunk, d, hg):
        r0 = pl.multiple_of(chunk * c, c)
        l0 = hg * GROUP_W
        slot = slot_of(d, hg, chunk)
        sidx = d * N_GROUPS + hg
        state = state_ref[sidx]
        r2 = _dot(wq_ref[slot], state.astype(BF16))
        yield
        vnew = u_ref[slot] - r2[:c]
        o = r2[c:] + _dot(attn_ref[slot], _block_diag(vnew, bd_mask))
        upd = _dot_tn(kg_ref[slot], vnew.astype(BF16))
        yield
        o_ref[0, pl.ds(r0, c), l0:l0 + GROUP_W] = o.astype(o_ref.dtype)
        end_row = r0 + (c - 1 if d == 0 else 0)
        etot = jnp.exp(gc_ref[d, pl.ds(end_row, 1), l0:l0 + GROUP_W])
        state_ref[sidx] = state * etot + upd * bd_f32

    pair = 4

    def prep_body(i, carry):
        gens = []
        for sub in range(pair):
            chunk = i * pair + sub
            for hg in range(N_GROUPS):
                gens.append(prep(qkvf_ref, chunk, 0, hg))
                gens.append(prep(qkvb_ref, chunk, 1, hg))
        _round_robin(gens)
        return carry

    lax.fori_loop(0, n_chunks // pair, prep_body, 0)

    def scan_body(ci, carry):
        gens = []
        for hg in range(N_GROUPS):
            gens.append(scan(of_ref, ci, 0, hg))
            gens.append(scan(ob_ref, n_chunks - 1 - ci, 1, hg))
        _round_robin(gens)
        return carry

    lax.fori_loop(0, n_chunks, scan_body, 0)


def _delta(qkvn, gates3, alog_row, dtb_row, consts, ts):
    b, s, _ = qkvn.shape
    nj = s // ts
    n_slots = 2 * N_GROUPS * (ts // DN_CHUNK)
    fwd = lambda i, j: (i, j, 0)
    bwd = lambda i, j: (i, nj - 1 - j, 0)
    return pl.pallas_call(
        _delta_kernel,
        grid=(b, nj),
        in_specs=[
            pl.BlockSpec((1, ts, DN_CONV_CH), fwd),
            pl.BlockSpec((1, ts, DN_CONV_CH), bwd),
            pl.BlockSpec((1, ts, GATE_PAD), fwd),
            pl.BlockSpec((1, ts, GATE_PAD), bwd),
            _const_spec((1, GATE_PAD)),
            _const_spec((1, GATE_PAD)),
            _const_spec((2, ts, ts)),
            _const_spec((4, GATE_PAD, DN_QK)),
            _const_spec((GROUP_W, GROUP_W)),
        ],
        out_specs=[
            pl.BlockSpec((1, ts, DN_V), fwd),
            pl.BlockSpec((1, ts, DN_V), bwd),
        ],
        out_shape=[
            jax.ShapeDtypeStruct((b, s, DN_V), BF16),
            jax.ShapeDtypeStruct((b, s, DN_V), BF16),
        ],
        scratch_shapes=[
            pltpu.VMEM((2, ts, DN_QK), F32),
            pltpu.VMEM((2, ts, DN_QK), F32),
            pltpu.VMEM((n_slots, DN_CHUNK, GROUP_W), F32),
            pltpu.VMEM((n_slots, 2 * DN_CHUNK, GROUP_W), BF16),
            pltpu.VMEM((n_slots, DN_CHUNK, GROUP_W), BF16),
            pltpu.VMEM((n_slots, DN_CHUNK, GROUP_W), BF16),
            pltpu.VMEM((2 * N_GROUPS, GROUP_W, GROUP_W), F32),
        ],
        compiler_params=_cparams(("parallel", "arbitrary")),
        name="gated_delta",
    )(qkvn, qkvn, gates3, gates3, alog_row, dtb_row,
      consts["tri"], consts["esel"], consts["bd"])


def _attn_kernel(q_ref, kvc_ref, kvp_ref, kvn_ref, sink_ref, o_ref, kv_ref, bias_ref):
    j = pl.program_id(1)
    nj = pl.num_programs(1)
    tq = q_ref.shape[1]
    blk = ATTN_BLOCK
    span = blk + 2 * WINDOW
    n_sub = tq // blk
    n_heads = 2 * ATTN_GROUP

    def head_of(idx):
        return idx // 2 + ATTN_GROUP * (idx % 2)

    @pl.when(j == 0)
    def _():
        qi = lax.broadcasted_iota(jnp.int32, (blk, span), 0)
        kr = lax.broadcasted_iota(jnp.int32, (blk, span), 1)
        dist_i = jnp.abs(qi + WINDOW - kr)
        dist = dist_i.astype(F32)
        for idx in range(n_heads):
            slope = 2.0 ** (-8.0 * (head_of(idx) + 1) / ATTN_HEADS)
            bias_ref[idx] = jnp.where(dist_i <= WINDOW, -slope * dist, -jnp.inf)

    kv_ref[0:WINDOW] = kvp_ref[0]
    kv_ref[WINDOW:WINDOW + tq] = kvc_ref[0]
    kv_ref[WINDOW + tq:] = kvn_ref[0]

    lane = lax.broadcasted_iota(jnp.int32, (1, LANES), 1)
    lo = lane < ATTN_HD
    kcol = lax.broadcasted_iota(jnp.int32, (1, span), 1)
    first_ok = jnp.logical_or(j > 0, kcol >= WINDOW)
    last_ok = jnp.logical_or(j < nj - 1, kcol < WINDOW + blk)
    zero = jnp.zeros((), BF16)
    orow = lax.broadcasted_iota(jnp.int32, (2 * span, LANES), 0)
    olane = lax.broadcasted_iota(jnp.int32, (2 * span, LANES), 1)
    ones_sel = ((orow < span) == (olane < ATTN_HD)).astype(BF16)

    for sb in range(n_sub):
        kwin = kv_ref[sb * blk:sb * blk + span, 0:ATTN_KV]
        vwin = kv_ref[sb * blk:sb * blk + span, ATTN_KV:2 * ATTN_KV]
        qs = []
        for g in range(ATTN_GROUP):
            qg = q_ref[0, sb * blk:(sb + 1) * blk, g * LANES:(g + 1) * LANES]
            qs.append(jnp.where(lo, qg, zero))
            qs.append(jnp.where(lo, zero, qg))
        s_all = _dot_nt(jnp.concatenate(qs, axis=0), kwin)
        es, sink_terms = [], []
        for idx in range(n_heads):
            head = head_of(idx)
            sink = sink_ref[head:head + 1, 0:1]
            sc = s_all[idx * blk:(idx + 1) * blk] + bias_ref[idx]
            if sb == 0:
                sc = jnp.where(first_ok, sc, -jnp.inf)
            if sb == n_sub - 1:
                sc = jnp.where(last_ok, sc, -jnp.inf)
            tiles = [sc[:, t * LANES:(t + 1) * LANES] for t in range(span // LANES)]
            m = jnp.max(functools.reduce(jnp.maximum, tiles), axis=-1, keepdims=True)
            m_b = jnp.broadcast_to(jnp.maximum(m, sink), (blk, LANES))
            es.append(jnp.concatenate([jnp.exp(t - m_b) for t in tiles], axis=1).astype(BF16))
            sink_terms.append(jnp.exp(sink - m_b))
        e_all = jnp.concatenate(
            [jnp.concatenate([es[2 * g], es[2 * g + 1]], axis=1) for g in range(ATTN_GROUP)],
            axis=0)
        v_stack = jnp.concatenate([jnp.where(lo, vwin, zero), jnp.where(lo, zero, vwin)], axis=0)
        nd = _dot(e_all, jnp.concatenate([v_stack, ones_sel], axis=1))
        for g in range(ATTN_GROUP):
            rows = slice(g * blk, (g + 1) * blk)
            den = nd[rows, LANES:] + jnp.where(lo, sink_terms[2 * g], sink_terms[2 * g + 1])
            o_ref[0, sb * blk:(sb + 1) * blk, g * LANES:(g + 1) * LANES] = (
                nd[rows, :LANES] / den).astype(BF16)


def _attn(p3, sink_b, q_col, kv_col, tq):
    b, s, _ = p3.shape
    nh = tq // WINDOW
    last = s // WINDOW - 1
    kvw = 2 * ATTN_KV
    span = ATTN_BLOCK + 2 * WINDOW
    return pl.pallas_call(
        _attn_kernel,
        grid=(b, s // tq),
        in_specs=[
            pl.BlockSpec((1, tq, ATTN_Q), lambda i, j: (i, j, q_col // ATTN_Q)),
            pl.BlockSpec((1, tq, kvw), lambda i, j: (i, j, kv_col // kvw)),
            pl.BlockSpec((1, WINDOW, kvw), lambda i, j: (i, jnp.maximum(j * nh - 1, 0), kv_col // kvw)),
            pl.BlockSpec((1, WINDOW, kvw), lambda i, j: (i, jnp.minimum((j + 1) * nh, last), kv_col // kvw)),
            _const_spec((ATTN_HEADS, LANES)),
        ],
        out_specs=pl.BlockSpec((1, tq, ATTN_Q), lambda i, j: (i, j, 0)),
        out_shape=jax.ShapeDtypeStruct((b, s, ATTN_Q), BF16),
        scratch_shapes=[
            pltpu.VMEM((tq + 2 * WINDOW, kvw), BF16),
            pltpu.VMEM((ATTN_HEADS, ATTN_BLOCK, span), F32),
        ],
        compiler_params=_cparams(("parallel", "arbitrary")),
        name="window_attn",
    )(p3, p3, p3, p3, sink_b)


def _merge_kernel(of_ref, ob_ref, z_ref, oatt_ref, bg_ref, x_ref,
                  dnw_ref, ones_ref, wa_ref, wb_ref, wo_ref, nw_ref, out_ref):
    d_model = x_ref.shape[1]
    oa = of_ref[...].astype(F32) + ob_ref[...].astype(F32)
    ms = _sel_dot(oa * oa, ones_ref[...], 2) * (1.0 / DN_DV)
    z = z_ref[...].astype(F32)
    on = oa * lax.rsqrt(ms + NORM_EPS) * dnw_ref[...] * (z * jax.nn.sigmoid(z))
    ya = _dot(on.astype(BF16), wa_ref[...])
    yb = _dot(oatt_ref[...], wb_ref[...])
    ga = jax.nn.sigmoid(bg_ref[:, 0:d_model].astype(F32))
    gb = jax.nn.sigmoid(bg_ref[:, d_model:2 * d_model].astype(F32))
    mix = _dot((ga * ya + gb * yb).astype(BF16), wo_ref[...])
    out_ref[...] = x_ref[...] + _rms(mix, nw_ref[...])


def _merge(o_f, o_b, p, o_att, xf, dnw, ones_hd, wa, wb, wo, nw, cols, tm):
    m, d = xf.shape
    row = lambda i: (i, 0)
    return pl.pallas_call(
        _merge_kernel,
        grid=(m // tm,),
        in_specs=[
            pl.BlockSpec((tm, DN_V), row),
            pl.BlockSpec((tm, DN_V), row),
            pl.BlockSpec((tm, DN_V), lambda i: (i, cols["z"] // DN_V)),
            pl.BlockSpec((tm, ATTN_Q), row),
            pl.BlockSpec((tm, 2 * d), lambda i: (i, cols["bg"] // (2 * d))),
            pl.BlockSpec((tm, d), row),
            _const_spec((1, DN_V)),
            _const_spec((DN_V, DN_V)),
            _const_spec((DN_V, d)),
            _const_spec((ATTN_Q, d)),
            _const_spec((d, d)),
            _const_spec((1, d)),
        ],
        out_specs=pl.BlockSpec((tm, d), row),
        out_shape=jax.ShapeDtypeStruct((m, d), F32),
        compiler_params=_cparams(("parallel",)),
        name="merge_out",
    )(o_f, o_b, p, o_att, p, xf, dnw, ones_hd, wa, wb, wo, nw)


def _mlp_kernel(x_ref, npre_ref, w1_ref, w2_ref, npost_ref, out_ref):
    x = x_ref[...]
    h = _rms(x, npre_ref[...]).astype(BF16)
    d_ff = w1_ref.shape[1]
    step = 1024
    acc = None
    for c0 in range(0, d_ff, step):
        u = jnp.maximum(_dot(h, w1_ref[:, c0:c0 + step]), 0.0)
        t = _dot((u * u).astype(BF16), w2_ref[c0:c0 + step, :])
        acc = t if acc is None else acc + t
    out_ref[...] = x + _rms(acc, npost_ref[...])


def _mlp(xf, npre, w1, w2, npost, tm):
    m, d = xf.shape
    d_ff = w1.shape[1]
    row = lambda i: (i, 0)
    single = pl.Buffered(1)
    return pl.pallas_call(
        _mlp_kernel,
        grid=(m // tm,),
        in_specs=[
            pl.BlockSpec((tm, d), row),
            _const_spec((1, d)),
            pl.BlockSpec((d, d_ff), lambda i: (0, 0), pipeline_mode=single),
            pl.BlockSpec((d_ff, d), lambda i: (0, 0), pipeline_mode=single),
            _const_spec((1, d)),
        ],
        out_specs=pl.BlockSpec((tm, d), row),
        out_shape=jax.ShapeDtypeStruct((m, d), F32),
        compiler_params=_cparams(("parallel",)),
        name="mlp",
    )(xf, npre, w1, w2, npost)


def _constants(ts):
    idx = jnp.arange(ts)
    same_chunk = (idx[:, None] // DN_CHUNK) == (idx[None, :] // DN_CHUNK)
    tri = jnp.stack([jnp.logical_and(same_chunk, idx[:, None] >= idx[None, :]),
                     jnp.logical_and(same_chunk, idx[:, None] <= idx[None, :])]).astype(BF16)
    lanes = jnp.arange(DN_QK) // DN_DK
    gate_rows = jnp.arange(GATE_PAD)
    esel = jnp.stack([(gate_rows[:, None] == (off + lanes)[None, :]).astype(BF16)
                      for off in (0, DN_HEADS, 2 * DN_HEADS, 3 * DN_HEADS)])
    gi = jnp.arange(GROUP_W) // DN_DK
    bd = (gi[:, None] == gi[None, :]).astype(BF16)
    hi = jnp.arange(DN_QK) // DN_DK
    ones_hd = (hi[:, None] == hi[None, :]).astype(BF16)
    return {"tri": tri, "esel": esel, "bd": bd, "ones_hd": ones_hd}


def _pick(total, want):
    t = min(want, total)
    while total % t:
        t //= 2
    return t


def kernel(x, w_in, conv_w, a_log, dt_bias, dn_norm_w, attn_sink, w_up_a, w_up_b, w_out,
           norm_mix_pre, norm_mix_post, norm_mlp_pre, norm_mlp_post, w_mlp_in, w_mlp_out):
    b, s, d = x.shape
    depth = w_in.shape[0]
    m = b * s
    tm = _pick(m, 512)
    ts_conv = _pick(s, 512)
    ts_delta = _pick(s, 256)
    tq = _pick(s, 512)
    consts = _constants(ts_delta)

    o_dz = DN_CONV_CH
    o_g = o_dz + DN_V
    o_aq = o_g + N_GATES
    o_ak = o_aq + ATTN_Q
    o_bg = o_ak + 2 * ATTN_KV
    cols = {"qkv": 0, "z": DN_CONV_CH, "bg": 2 * d, "aq": 4 * d, "akv": 4 * d + ATTN_Q}
    head_order = [h for g in range(ATTN_GROUP) for h in (g, g + ATTN_GROUP)]
    aq_perm = jnp.concatenate([jnp.arange(h * ATTN_HD, (h + 1) * ATTN_HD) for h in head_order])

    xf = x.reshape(m, d)
    for l in range(depth):
        wl = w_in[l]
        aq_cols = wl[:, o_aq:o_aq + ATTN_Q][:, aq_perm] * (ATTN_HD ** -0.5)
        w_main = jnp.concatenate(
            [wl[:, 0:o_g], wl[:, o_bg:o_bg + 2 * d], aq_cols, wl[:, o_ak:o_ak + 2 * ATTN_KV]],
            axis=1).astype(BF16)
        w_gate = jnp.pad(wl[:, o_g:o_g + N_GATES], ((0, 0), (0, GATE_PAD - N_GATES))).astype(BF16)
        row = lambda v: v.reshape(1, -1).astype(F32)

        p, gates = _in_proj(xf, row(norm_mix_pre[l]), w_main, w_gate, tm)
        p3 = p.reshape(b, s, p.shape[1])
        qkvn = _conv(p3, conv_w[l].astype(F32), consts["ones_hd"], ts_conv)

        alog_row = jnp.pad(a_log[l].reshape(1, -1).astype(F32), ((0, 0), (0, GATE_PAD - 2 * DN_HEADS)))
        dtb_row = jnp.pad(dt_bias[l].reshape(1, -1).astype(F32), ((0, 0), (0, GATE_PAD - 2 * DN_HEADS)))
        o_f, o_b = _delta(qkvn, gates.reshape(b, s, GATE_PAD), alog_row, dtb_row, consts, ts_delta)

        sink_b = jnp.broadcast_to(attn_sink[l].astype(F32)[:, None], (ATTN_HEADS, LANES))
        o_att = _attn(p3, sink_b, cols["aq"], cols["akv"], tq)

        dnw = jnp.tile(dn_norm_w[l].astype(F32), DN_HEADS).reshape(1, DN_V)
        xf = _merge(o_f.reshape(m, DN_V), o_b.reshape(m, DN_V), p, o_att.reshape(m, ATTN_Q), xf,
                    dnw, consts["ones_hd"], w_up_a[l].astype(BF16),
                    w_up_b[l][aq_perm, :].astype(BF16), w_out[l].astype(BF16),
                    row(norm_mix_post[l]), cols, tm)
        xf = _mlp(xf, row(norm_mlp_pre[l]), w_mlp_in[l].astype(BF16), w_mlp_out[l].astype(BF16),
                  row(norm_mlp_post[l]), tm)
    return xf.reshape(b, s, d)
```

```python
import functools

import jax
import jax.numpy as jnp
from jax import lax
from jax.experimental import pallas as pl
from jax.experimental.pallas import tpu as pltpu

F32 = jnp.float32
BF16 = jnp.bfloat16

NORM_EPS = 1e-6
DN_HEADS = 8
DN_DK = 64
DN_DV = 64
DN_CONV = 5
DN_CHUNK = 64
ATTN_HEADS = 8
ATTN_KV_HEADS = 2
ATTN_GROUP = ATTN_HEADS // ATTN_KV_HEADS
ATTN_HD = 64
WINDOW = 128
ATTN_BLOCK = 128

DN_QK = DN_HEADS * DN_DK
DN_V = DN_HEADS * DN_DV
DN_CONV_CH = 2 * DN_QK + DN_V
ATTN_Q = ATTN_HEADS * ATTN_HD
ATTN_KV = ATTN_KV_HEADS * ATTN_HD
N_GATES = 4 * DN_HEADS

LANES = 128
GATE_PAD = LANES
HEADS_PER_GROUP = 4
GROUP_W = HEADS_PER_GROUP * DN_DK
N_GROUPS = DN_HEADS // HEADS_PER_GROUP

VMEM_LIMIT = 56 * 1024 * 1024


def _cparams(sem):
    return pltpu.CompilerParams(dimension_semantics=sem, vmem_limit_bytes=VMEM_LIMIT)


def _const_spec(shape):
    nd = len(shape)
    return pl.BlockSpec(shape, lambda *_: (0,) * nd)


def _dot(a, b):
    return jnp.dot(a, b, preferred_element_type=F32)


def _dot_nt(a, b):
    return lax.dot_general(a, b, (((1,), (1,)), ((), ())), preferred_element_type=F32)


def _dot_tn(a, b):
    return lax.dot_general(a, b, (((0,), (0,)), ((), ())), preferred_element_type=F32)


def _split_terms(x, terms):
    out, r = [], x
    for t in range(terms):
        hi = r.astype(BF16)
        out.append(hi)
        if t + 1 < terms:
            r = r - hi.astype(F32)
    return out


def _sel_dot(x, sel, terms):
    acc = None
    for piece in _split_terms(x, terms):
        d = _dot(piece, sel)
        acc = d if acc is None else acc + d
    return acc


def _sel_dot_left(sel, x, terms):
    acc = None
    for piece in _split_terms(x, terms):
        d = _dot(sel, piece)
        acc = d if acc is None else acc + d
    return acc


def _rms(x, w):
    ms = jnp.mean(x * x, axis=-1, keepdims=True)
    return x * lax.rsqrt(ms + NORM_EPS) * w


X_HALO = 8


def _in_proj_kernel(x_ref, xp_ref, xn_ref, nw_ref, wqkv_ref, w_ref, wg_ref, cw_ref, ones_ref,
                    qkv_ref, p_ref, g_ref, ext_ref, *, tiles_per_seq):
    jj = pl.program_id(0) % tiles_per_seq
    tm = x_ref.shape[0]
    pad = DN_CONV // 2
    x_ext = jnp.concatenate([xp_ref[...], x_ref[...], xn_ref[...]], axis=0)
    h_ext = _rms(x_ext, nw_ref[...])
    pre = _dot(h_ext.astype(BF16), wqkv_ref[...])
    ext_ref[...] = pre
    ext_ref[0:X_HALO, :] = jnp.where(jj > 0, pre[0:X_HALO], 0.0)
    ext_ref[X_HALO + tm:, :] = jnp.where(jj < tiles_per_seq - 1, pre[X_HALO + tm:], 0.0)

    h = h_ext[X_HALO:X_HALO + tm].astype(BF16)
    n = w_ref.shape[1]
    step = 512
    for c0 in range(0, n, step):
        c1 = min(c0 + step, n)
        p_ref[:, c0:c1] = _dot(h, w_ref[:, c0:c1]).astype(BF16)
    g_ref[...] = _dot(h, wg_ref[...])

    ones = ones_ref[...]

    def conv_silu(c0, c1):
        acc = None
        for k in range(DN_CONV):
            off = X_HALO - pad + k
            term = ext_ref[off:off + tm, c0:c1] * cw_ref[k:k + 1, c0:c1]
            acc = term if acc is None else acc + term
        return acc * jax.nn.sigmoid(acc)

    def l2n(t, scale):
        ss = _sel_dot(t * t, ones, 2)
        return t * (lax.rsqrt(ss + NORM_EPS) * scale)

    qkv_ref[:, 0:DN_QK] = l2n(conv_silu(0, DN_QK), DN_DK ** -0.5).astype(BF16)
    qkv_ref[:, DN_QK:2 * DN_QK] = l2n(conv_silu(DN_QK, 2 * DN_QK), 1.0).astype(BF16)
    qkv_ref[:, 2 * DN_QK:] = conv_silu(2 * DN_QK, DN_CONV_CH).astype(BF16)


def _in_proj(xf, nw, wqkv, w, wg, cw, ones_hd, tm, s):
    m, d = xf.shape
    n = w.shape[1]
    nh = tm // X_HALO
    last = m // X_HALO - 1
    single = pl.Buffered(1)
    return pl.pallas_call(
        functools.partial(_in_proj_kernel, tiles_per_seq=s // tm),
        grid=(m // tm,),
        in_specs=[
            pl.BlockSpec((tm, d), lambda i: (i, 0)),
            pl.BlockSpec((X_HALO, d), lambda i: (jnp.maximum(i * nh - 1, 0), 0)),
            pl.BlockSpec((X_HALO, d), lambda i: (jnp.minimum((i + 1) * nh, last), 0)),
            _const_spec((1, d)),
            pl.BlockSpec((d, DN_CONV_CH), lambda i: (0, 0), pipeline_mode=single),
            pl.BlockSpec((d, n), lambda i: (0, 0), pipeline_mode=single),
            _const_spec((d, GATE_PAD)),
            _const_spec((DN_CONV, DN_CONV_CH)),
            _const_spec((DN_QK, DN_QK)),
        ],
        out_specs=[
            pl.BlockSpec((tm, DN_CONV_CH), lambda i: (i, 0)),
            pl.BlockSpec((tm, n), lambda i: (i, 0)),
            pl.BlockSpec((tm, GATE_PAD), lambda i: (i, 0)),
        ],
        out_shape=[
            jax.ShapeDtypeStruct((m, DN_CONV_CH), BF16),
            jax.ShapeDtypeStruct((m, n), BF16),
            jax.ShapeDtypeStruct((m, GATE_PAD), F32),
        ],
        scratch_shapes=[pltpu.VMEM((tm + 2 * X_HALO, DN_CONV_CH), F32)],
        compiler_params=_cparams(("parallel",)),
        name="in_proj",
    )(xf, xf, xf, nw, wqkv, w, wg, cw, ones_hd)


def _softplus(x):
    return jnp.maximum(x, 0.0) + jnp.log1p(jnp.exp(-jnp.abs(x)))


def _block_diag(x, half_masks):
    xb = x.astype(BF16)
    zeros = jnp.zeros((xb.shape[0], LANES), BF16)
    blocks = []
    for h in range(HEADS_PER_GROUP):
        tile, half = divmod(h, 2)
        kept = xb[:, tile * LANES:(tile + 1) * LANES] * half_masks[half]
        blocks.append(jnp.concatenate([kept, zeros] if tile == 0 else [zeros, kept], axis=1))
    return jnp.concatenate(blocks, axis=0)


def _mask_block_diag(bd, tile_mask):
    c = tile_mask.shape[0]
    zeros = jnp.zeros((c, LANES), BF16)
    blocks = []
    for h in range(HEADS_PER_GROUP):
        tile = h // 2
        kept = bd[h * c:(h + 1) * c, tile * LANES:(tile + 1) * LANES] * tile_mask
        blocks.append(jnp.concatenate([kept, zeros] if tile == 0 else [zeros, kept], axis=1))
    return jnp.concatenate(blocks, axis=0)


def _round_robin(gens):
    live = list(gens)
    while live:
        nxt = []
        for g in live:
            try:
                next(g)
                nxt.append(g)
            except StopIteration:
                pass
        live = nxt


def _delta_kernel(qkvf_ref, qkvb_ref, gf_ref, gb_ref, alog_ref, dtb_ref,
                  tri_ref, esel_ref, bd_ref,
                  of_ref, ob_ref,
                  gc_ref, beta_ref, u_ref, wq_ref, attn_ref, kg_ref, state_ref):
    j = pl.program_id(1)
    bb, ts = qkvf_ref.shape[0], qkvf_ref.shape[1]
    n_chunks = ts // DN_CHUNK
    c = DN_CHUNK

    @pl.when(j == 0)
    def _():
        state_ref[...] = jnp.zeros_like(state_ref)

    lane_g = lax.broadcasted_iota(jnp.int32, (1, GATE_PAD), 1)
    rate = jnp.where(lane_g < 2 * DN_HEADS, jnp.exp(alog_ref[...]), 0.0)
    dtb = dtb_ref[...]

    for bi in range(bb):
        for d, g_ref in ((0, gf_ref), (1, gb_ref)):
            gates = g_ref[bi]
            g = -rate * _softplus(gates + dtb)
            gcd = _sel_dot_left(tri_ref[d], g, 2)
            gc_ref[2 * bi + d] = _sel_dot(gcd, esel_ref[d], 2)
            beta_ref[2 * bi + d] = _sel_dot(jax.nn.sigmoid(gates), esel_ref[2 + d], 1)

    bd_f32 = bd_ref[...].astype(F32)
    lane_t = lax.broadcasted_iota(jnp.int32, (c, LANES), 1)
    half_masks = ((lane_t < DN_DK).astype(BF16), (lane_t >= DN_DK).astype(BF16))
    row = lax.broadcasted_iota(jnp.int32, (c, GROUP_W), 0)
    col = lax.broadcasted_iota(jnp.int32, (c, GROUP_W), 1) % c
    eye = row == col
    eye_f = eye.astype(F32)
    masks = ((row >= col, row > col), (row <= col, row < col))
    base = 8
    same_blk = lambda bs: (row // bs) == (col // bs)
    base_blk = same_blk(base)
    merge_masks = []
    bs = base
    while bs < c:
        off_blk = jnp.logical_and(same_blk(2 * bs), jnp.logical_not(same_blk(bs)))
        merge_masks.append(off_blk[:, :LANES].astype(BF16))
        bs *= 2

    def slot_of(bi, d, hg, chunk):
        return ((2 * bi + d) * N_GROUPS + hg) * n_chunks + chunk

    def prep(qkv_ref, bi, chunk, d, hg):
        rows = pl.ds(pl.multiple_of(chunk * c, c), c)
        l0 = hg * GROUP_W
        q = qkv_ref[bi, rows, l0:l0 + GROUP_W]
        k = qkv_ref[bi, rows, DN_QK + l0:DN_QK + l0 + GROUP_W]
        v = qkv_ref[bi, rows, 2 * DN_QK + l0:2 * DN_QK + l0 + GROUP_W]
        incl, strict = masks[d]
        slot = slot_of(bi, d, hg, chunk)

        bdk = _block_diag(k, half_masks)
        r = _dot_nt(jnp.concatenate([q, k], axis=0), bdk)
        yield
        gcc = gc_ref[2 * bi + d, rows, l0:l0 + GROUP_W]
        beta = beta_ref[2 * bi + d, rows, l0:l0 + GROUP_W]
        gcrow = jnp.sum(jnp.where(eye, gcc, 0.0), axis=0, keepdims=True)
        decay = jnp.where(incl, jnp.exp(jnp.where(incl, gcc - gcrow, 0.0)), 0.0)
        attn_ref[slot] = (r[:c] * decay).astype(BF16)
        nmat = jnp.where(strict, -(r[c:] * beta * decay), 0.0)

        n_base = jnp.where(base_blk, nmat, 0.0)
        p = eye_f + n_base
        n2 = _dot(n_base.astype(BF16), _block_diag(n_base, half_masks))
        yield
        r = _dot(jnp.concatenate([p, n2], axis=0).astype(BF16), _block_diag(n2, half_masks))
        yield
        p = p + r[:c]
        x = _dot(p.astype(BF16), _block_diag(r[c:], half_masks))
        yield
        p = p + x
        bdn = _block_diag(nmat, half_masks)
        for tile_mask in merge_masks:
            pb = p.astype(BF16)
            t = _dot(pb, _mask_block_diag(bdn, tile_mask))
            yield
            x = _dot(t.astype(BF16), _block_diag(pb, half_masks))
            yield
            p = p + x

        kf, vf = k.astype(F32), v.astype(F32)
        egc = jnp.exp(gcc)
        g_end = gcc[c - 1:c, :] if d == 0 else gcc[0:1, :]
        rhs = jnp.concatenate([_block_diag(vf * beta, half_masks),
                               _block_diag(kf * (beta * egc), half_masks)], axis=1)
        uw = _dot(p.astype(BF16), rhs)
        yield
        u_ref[slot] = uw[:, :GROUP_W]
        wq_ref[slot, 0:c, :] = uw[:, GROUP_W:].astype(BF16)
        wq_ref[slot, c:2 * c, :] = (q.astype(F32) * egc).astype(BF16)
        kg_ref[slot] = (kf * jnp.exp(g_end - gcc)).astype(BF16)

    def scan(o_ref, bi, chunk, d, hg):
        r0 = pl.multiple_of(chunk * c, c)
        l0 = hg * GROUP_W
        slot = slot_of(bi, d, hg, chunk)
        sidx = (2 * bi + d) * N_GROUPS + hg
        state = state_ref[sidx]
        r2 = _dot(wq_ref[slot], state.astype(BF16))
        yield
        vnew = u_ref[slot] - r2[:c]
        o = r2[c:] + _dot(attn_ref[slot], _block_diag(vnew, half_masks))
        upd = _dot_tn(kg_ref[slot], vnew.astype(BF16))
        yield
        o_ref[bi, pl.ds(r0, c), l0:l0 + GROUP_W] = o.astype(o_ref.dtype)
        end_row = r0 + (c - 1 if d == 0 else 0)
        etot = jnp.exp(gc_ref[2 * bi + d, pl.ds(end_row, 1), l0:l0 + GROUP_W])
        state_ref[sidx] = state * etot + upd * bd_f32

    def prep_body(bi, carry):
        gens = []
        for chunk in range(n_chunks):
            for hg in range(N_GROUPS):
                gens.append(prep(qkvf_ref, bi, chunk, 0, hg))
                gens.append(prep(qkvb_ref, bi, chunk, 1, hg))
        _round_robin(gens)
        return carry

    lax.fori_loop(0, bb, prep_body, 0)

    def scan_body(ci, carry):
        gens = []
        for bi in range(bb):
            for hg in range(N_GROUPS):
                gens.append(scan(of_ref, bi, ci, 0, hg))
                gens.append(scan(ob_ref, bi, n_chunks - 1 - ci, 1, hg))
        _round_robin(gens)
        return carry

    lax.fori_loop(0, n_chunks, scan_body, 0)


def _delta(qkvn, gates3, alog_row, dtb_row, consts, ts):
    b, s, _ = qkvn.shape
    nj = s // ts
    bb = 2 if b % 2 == 0 else 1
    n_slots = bb * 2 * N_GROUPS * (ts // DN_CHUNK)
    fwd = lambda i, j: (i, j, 0)
    bwd = lambda i, j: (i, nj - 1 - j, 0)
    return pl.pallas_call(
        _delta_kernel,
        grid=(b // bb, nj),
        in_specs=[
            pl.BlockSpec((bb, ts, DN_CONV_CH), fwd),
            pl.BlockSpec((bb, ts, DN_CONV_CH), bwd),
            pl.BlockSpec((bb, ts, GATE_PAD), fwd),
            pl.BlockSpec((bb, ts, GATE_PAD), bwd),
            _const_spec((1, GATE_PAD)),
            _const_spec((1, GATE_PAD)),
            _const_spec((2, ts, ts)),
            _const_spec((4, GATE_PAD, DN_QK)),
            _const_spec((GROUP_W, GROUP_W)),
        ],
        out_specs=[
            pl.BlockSpec((bb, ts, DN_V), fwd),
            pl.BlockSpec((bb, ts, DN_V), bwd),
        ],
        out_shape=[
            jax.ShapeDtypeStruct((b, s, DN_V), BF16),
            jax.ShapeDtypeStruct((b, s, DN_V), BF16),
        ],
        scratch_shapes=[
            pltpu.VMEM((2 * bb, ts, DN_QK), F32),
            pltpu.VMEM((2 * bb, ts, DN_QK), F32),
            pltpu.VMEM((n_slots, DN_CHUNK, GROUP_W), F32),
            pltpu.VMEM((n_slots, 2 * DN_CHUNK, GROUP_W), BF16),
            pltpu.VMEM((n_slots, DN_CHUNK, GROUP_W), BF16),
            pltpu.VMEM((n_slots, DN_CHUNK, GROUP_W), BF16),
            pltpu.VMEM((2 * bb * N_GROUPS, GROUP_W, GROUP_W), F32),
        ],
        compiler_params=_cparams(("parallel", "arbitrary")),
        name="gated_delta",
    )(qkvn, qkvn, gates3, gates3, alog_row, dtb_row,
      consts["tri"], consts["esel"], consts["bd"])


def _attn_kernel(q_ref, kvc_ref, kvp_ref, kvn_ref, sink_ref, o_ref, kv_ref, bias_ref):
    j = pl.program_id(1)
    nj = pl.num_programs(1)
    tq = q_ref.shape[1]
    blk = ATTN_BLOCK
    span = blk + 2 * WINDOW
    n_sub = tq // blk
    n_heads = 2 * ATTN_GROUP

    def head_of(idx):
        return idx // 2 + ATTN_GROUP * (idx % 2)

    @pl.when(j == 0)
    def _():
        qi = lax.broadcasted_iota(jnp.int32, (blk, span), 0)
        kr = lax.broadcasted_iota(jnp.int32, (blk, span), 1)
        dist_i = jnp.abs(qi + WINDOW - kr)
        dist = dist_i.astype(F32)
        for idx in range(n_heads):
            slope = 2.0 ** (-8.0 * (head_of(idx) + 1) / ATTN_HEADS)
            bias_ref[idx] = jnp.where(dist_i <= WINDOW, -slope * dist, -jnp.inf)

    kv_ref[0:WINDOW] = kvp_ref[0]
    kv_ref[WINDOW:WINDOW + tq] = kvc_ref[0]
    kv_ref[WINDOW + tq:] = kvn_ref[0]

    lane = lax.broadcasted_iota(jnp.int32, (1, LANES), 1)
    lo = lane < ATTN_HD
    kcol = lax.broadcasted_iota(jnp.int32, (1, span), 1)
    first_ok = jnp.logical_or(j > 0, kcol >= WINDOW)
    last_ok = jnp.logical_or(j < nj - 1, kcol < WINDOW + blk)
    zero = jnp.zeros((), BF16)
    orow = lax.broadcasted_iota(jnp.int32, (2 * span, LANES), 0)
    olane = lax.broadcasted_iota(jnp.int32, (2 * span, LANES), 1)
    ones_sel = ((orow < span) == (olane < ATTN_HD)).astype(BF16)

    for sb in range(n_sub):
        kwin = kv_ref[sb * blk:sb * blk + span, 0:ATTN_KV]
        vwin = kv_ref[sb * blk:sb * blk + span, ATTN_KV:2 * ATTN_KV]
        qs = []
        for g in range(ATTN_GROUP):
            qg = q_ref[0, sb * blk:(sb + 1) * blk, g * LANES:(g + 1) * LANES]
            qs.append(jnp.where(lo, qg, zero))
            qs.append(jnp.where(lo, zero, qg))
        s_all = _dot_nt(jnp.concatenate(qs, axis=0), kwin)
        es, sink_terms = [], []
        for idx in range(n_heads):
            head = head_of(idx)
            sink = sink_ref[head:head + 1, 0:1]
            sc = s_all[idx * blk:(idx + 1) * blk] + bias_ref[idx]
            if sb == 0:
                sc = jnp.where(first_ok, sc, -jnp.inf)
            if sb == n_sub - 1:
                sc = jnp.where(last_ok, sc, -jnp.inf)
            tiles = [sc[:, t * LANES:(t + 1) * LANES] for t in range(span // LANES)]
            m = jnp.max(functools.reduce(jnp.maximum, tiles), axis=-1, keepdims=True)
            m_b = jnp.broadcast_to(jnp.maximum(m, sink), (blk, LANES))
            es.append(jnp.concatenate([jnp.exp(t - m_b) for t in tiles], axis=1).astype(BF16))
            sink_terms.append(jnp.exp(sink - m_b))
        e_all = jnp.concatenate(
            [jnp.concatenate([es[2 * g], es[2 * g + 1]], axis=1) for g in range(ATTN_GROUP)],
            axis=0)
        v_stack = jnp.concatenate([jnp.where(lo, vwin, zero), jnp.where(lo, zero, vwin)], axis=0)
        nd = _dot(e_all, jnp.concatenate([v_stack, ones_sel], axis=1))
        for g in range(ATTN_GROUP):
            rows = slice(g * blk, (g + 1) * blk)
            den = nd[rows, LANES:] + jnp.where(lo, sink_terms[2 * g], sink_terms[2 * g + 1])
            o_ref[0, sb * blk:(sb + 1) * blk, g * LANES:(g + 1) * LANES] = (
                nd[rows, :LANES] / den).astype(BF16)


def _attn(p3, sink_b, q_col, kv_col, tq):
    b, s, _ = p3.shape
    nh = tq // WINDOW
    last = s // WINDOW - 1
    kvw = 2 * ATTN_KV
    span = ATTN_BLOCK + 2 * WINDOW
    return pl.pallas_call(
        _attn_kernel,
        grid=(b, s // tq),
        in_specs=[
            pl.BlockSpec((1, tq, ATTN_Q), lambda i, j: (i, j, q_col // ATTN_Q)),
            pl.BlockSpec((1, tq, kvw), lambda i, j: (i, j, kv_col // kvw)),
            pl.BlockSpec((1, WINDOW, kvw), lambda i, j: (i, jnp.maximum(j * nh - 1, 0), kv_col // kvw)),
            pl.BlockSpec((1, WINDOW, kvw), lambda i, j: (i, jnp.minimum((j + 1) * nh, last), kv_col // kvw)),
            _const_spec((ATTN_HEADS, LANES)),
        ],
        out_specs=pl.BlockSpec((1, tq, ATTN_Q), lambda i, j: (i, j, 0)),
        out_shape=jax.ShapeDtypeStruct((b, s, ATTN_Q), BF16),
        scratch_shapes=[
            pltpu.VMEM((tq + 2 * WINDOW, kvw), BF16),
            pltpu.VMEM((ATTN_HEADS, ATTN_BLOCK, span), F32),
        ],
        compiler_params=_cparams(("parallel", "arbitrary")),
        name="window_attn",
    )(p3, p3, p3, p3, sink_b)


def _mix_mlp_kernel(of_ref, ob_ref, z_ref, oatt_ref, bg_ref, x_ref,
                    dnw_ref, ones_ref, wa_ref, wb_ref, wo_ref, nmix_ref,
                    npre_ref, w1_ref, w2_ref, npost_ref, out_ref):
    d_model = x_ref.shape[1]
    oa = of_ref[...].astype(F32) + ob_ref[...].astype(F32)
    ms = _sel_dot(oa * oa, ones_ref[...], 2) * (1.0 / DN_DV)
    z = z_ref[...].astype(F32)
    on = oa * lax.rsqrt(ms + NORM_EPS) * dnw_ref[...] * (z * jax.nn.sigmoid(z))
    ya = _dot(on.astype(BF16), wa_ref[...])
    yb = _dot(oatt_ref[...], wb_ref[...])
    ga = jax.nn.sigmoid(bg_ref[:, 0:d_model].astype(F32))
    gb = jax.nn.sigmoid(bg_ref[:, d_model:2 * d_model].astype(F32))
    mix = _dot((ga * ya + gb * yb).astype(BF16), wo_ref[...])
    x1 = x_ref[...] + _rms(mix, nmix_ref[...])

    h = _rms(x1, npre_ref[...]).astype(BF16)
    d_ff = w1_ref.shape[1]
    step = 1024
    acc = None
    for c0 in range(0, d_ff, step):
        u = jnp.maximum(_dot(h, w1_ref[:, c0:c0 + step]), 0.0)
        t = _dot((u * u).astype(BF16), w2_ref[c0:c0 + step, :])
        acc = t if acc is None else acc + t
    out_ref[...] = x1 + _rms(acc, npost_ref[...])


def _mix_mlp(o_f, o_b, p, o_att, xf, dnw, ones_hd, wa, wb, wo, nmix, npre, w1, w2, npost, cols, tm):
    m, d = xf.shape
    d_ff = w1.shape[1]
    row = lambda i: (i, 0)
    single = pl.Buffered(1)
    resident = lambda shape: pl.BlockSpec(shape, lambda i: (0, 0), pipeline_mode=single)
    return pl.pallas_call(
        _mix_mlp_kernel,
        grid=(m // tm,),
        in_specs=[
            pl.BlockSpec((tm, DN_V), row),
            pl.BlockSpec((tm, DN_V), row),
            pl.BlockSpec((tm, DN_V), lambda i: (i, cols["z"] // DN_V)),
            pl.BlockSpec((tm, ATTN_Q), row),
            pl.BlockSpec((tm, 2 * d), lambda i: (i, cols["bg"] // (2 * d))),
            pl.BlockSpec((tm, d), row),
            _const_spec((1, DN_V)),
            _const_spec((DN_V, DN_V)),
            resident((DN_V, d)),
            resident((ATTN_Q, d)),
            resident((d, d)),
            _const_spec((1, d)),
            _const_spec((1, d)),
            resident((d, d_ff)),
            resident((d_ff, d)),
            _const_spec((1, d)),
        ],
        out_specs=pl.BlockSpec((tm, d), row),
        out_shape=jax.ShapeDtypeStruct((m, d), F32),
        compiler_params=_cparams(("parallel",)),
        name="mix_mlp",
    )(o_f, o_b, p, o_att, p, xf, dnw, ones_hd, wa, wb, wo, nmix, npre, w1, w2, npost)


def _constants(ts):
    idx = jnp.arange(ts)
    same_chunk = (idx[:, None] // DN_CHUNK) == (idx[None, :] // DN_CHUNK)
    tri = jnp.stack([jnp.logical_and(same_chunk, idx[:, None] >= idx[None, :]),
                     jnp.logical_and(same_chunk, idx[:, None] <= idx[None, :])]).astype(BF16)
    lanes = jnp.arange(DN_QK) // DN_DK
    gate_rows = jnp.arange(GATE_PAD)
    esel = jnp.stack([(gate_rows[:, None] == (off + lanes)[None, :]).astype(BF16)
                      for off in (0, DN_HEADS, 2 * DN_HEADS, 3 * DN_HEADS)])
    gi = jnp.arange(GROUP_W) // DN_DK
    bd = (gi[:, None] == gi[None, :]).astype(BF16)
    hi = jnp.arange(DN_QK) // DN_DK
    ones_hd = (hi[:, None] == hi[None, :]).astype(BF16)
    return {"tri": tri, "esel": esel, "bd": bd, "ones_hd": ones_hd}


def _pick(total, want):
    t = min(want, total)
    while total % t:
        t //= 2
    return t


def kernel(x, w_in, conv_w, a_log, dt_bias, dn_norm_w, attn_sink, w_up_a, w_up_b, w_out,
           norm_mix_pre, norm_mix_post, norm_mlp_pre, norm_mlp_post, w_mlp_in, w_mlp_out):
    b, s, d = x.shape
    depth = w_in.shape[0]
    m = b * s
    tm = _pick(s, 512)
    ts_delta = _pick(s, 256)
    tq = _pick(s, 512)
    consts = _constants(ts_delta)

    o_dz = DN_CONV_CH
    o_g = o_dz + DN_V
    o_aq = o_g + N_GATES
    o_ak = o_aq + ATTN_Q
    o_bg = o_ak + 2 * ATTN_KV
    cols = {"bg": 0, "z": 2 * d, "aq": 2 * d + DN_V, "akv": 2 * d + DN_V + ATTN_Q}
    head_order = [h for g in range(ATTN_GROUP) for h in (g, g + ATTN_GROUP)]
    row = lambda v: v.reshape(1, -1).astype(F32)

    xf = x.reshape(m, d)
    for l in range(depth):
        wl = w_in[l]
        aq_cols = [wl[:, o_aq + h * ATTN_HD:o_aq + (h + 1) * ATTN_HD] * (ATTN_HD ** -0.5)
                   for h in head_order]
        w_main = jnp.concatenate(
            [wl[:, o_bg:o_bg + 2 * d], wl[:, o_dz:o_g]] + aq_cols + [wl[:, o_ak:o_ak + 2 * ATTN_KV]],
            axis=1).astype(BF16)
        w_qkv = wl[:, 0:DN_CONV_CH].astype(BF16)
        w_gate = jnp.pad(wl[:, o_g:o_g + N_GATES], ((0, 0), (0, GATE_PAD - N_GATES))).astype(BF16)
        wb_rows = jnp.concatenate(
            [w_up_b[l][h * ATTN_HD:(h + 1) * ATTN_HD] for h in head_order], axis=0).astype(BF16)

        qkvn, p, gates = _in_proj(xf, row(norm_mix_pre[l]), w_qkv, w_main, w_gate,
                                  conv_w[l].astype(F32), consts["ones_hd"], tm, s)

        alog_row = jnp.pad(a_log[l].reshape(1, -1).astype(F32), ((0, 0), (0, GATE_PAD - 2 * DN_HEADS)))
        dtb_row = jnp.pad(dt_bias[l].reshape(1, -1).astype(F32), ((0, 0), (0, GATE_PAD - 2 * DN_HEADS)))
        o_f, o_b = _delta(qkvn.reshape(b, s, DN_CONV_CH), gates.reshape(b, s, GATE_PAD),
                          alog_row, dtb_row, consts, ts_delta)

        sink_b = jnp.broadcast_to(attn_sink[l].astype(F32)[:, None], (ATTN_HEADS, LANES))
        o_att = _attn(p.reshape(b, s, p.shape[1]), sink_b, cols["aq"], cols["akv"], tq)

        dnw = jnp.tile(dn_norm_w[l].astype(F32), DN_HEADS).reshape(1, DN_V)
        xf = _mix_mlp(o_f.reshape(m, DN_V), o_b.reshape(m, DN_V), p, o_att.reshape(m, ATTN_Q), xf,
                      dnw, consts["ones_hd"], w_up_a[l].astype(BF16), wb_rows, w_out[l].astype(BF16),
                      row(norm_mix_post[l]), row(norm_mlp_pre[l]), w_mlp_in[l].astype(BF16),
                      w_mlp_out[l].astype(BF16), row(norm_mlp_post[l]), cols, tm)
    return xf.reshape(b, s, d)
```

```python
import functools

import jax
import jax.numpy as jnp
from jax import lax
from jax.experimental import pallas as pl
from jax.experimental.pallas import tpu as pltpu

F32 = jnp.float32
BF16 = jnp.bfloat16

NORM_EPS = 1e-6
DN_HEADS = 8
DN_DK = 64
DN_DV = 64
DN_CONV = 5
DN_CHUNK = 64
ATTN_HEADS = 8
ATTN_KV_HEADS = 2
ATTN_GROUP = ATTN_HEADS // ATTN_KV_HEADS
ATTN_HD = 64
WINDOW = 128
ATTN_BLOCK = 128

DN_QK = DN_HEADS * DN_DK
DN_V = DN_HEADS * DN_DV
DN_CONV_CH = 2 * DN_QK + DN_V
ATTN_Q = ATTN_HEADS * ATTN_HD
ATTN_KV = ATTN_KV_HEADS * ATTN_HD
N_GATES = 4 * DN_HEADS

LANES = 128
GATE_PAD = LANES
HEADS_PER_GROUP = 4
GROUP_W = HEADS_PER_GROUP * DN_DK
N_GROUPS = DN_HEADS // HEADS_PER_GROUP

VMEM_LIMIT = 56 * 1024 * 1024


def _cparams(sem):
    return pltpu.CompilerParams(dimension_semantics=sem, vmem_limit_bytes=VMEM_LIMIT)


def _const_spec(shape):
    nd = len(shape)
    return pl.BlockSpec(shape, lambda *_: (0,) * nd)


def _dot(a, b):
    return jnp.dot(a, b, preferred_element_type=F32)


def _dot_nt(a, b):
    return lax.dot_general(a, b, (((1,), (1,)), ((), ())), preferred_element_type=F32)


def _dot_tn(a, b):
    return lax.dot_general(a, b, (((0,), (0,)), ((), ())), preferred_element_type=F32)


def _split_terms(x, terms):
    out, r = [], x
    for t in range(terms):
        hi = r.astype(BF16)
        out.append(hi)
        if t + 1 < terms:
            r = r - hi.astype(F32)
    return out


def _sel_dot(x, sel, terms):
    acc = None
    for piece in _split_terms(x, terms):
        d = _dot(piece, sel)
        acc = d if acc is None else acc + d
    return acc


def _sel_dot_left(sel, x, terms):
    acc = None
    for piece in _split_terms(x, terms):
        d = _dot(sel, piece)
        acc = d if acc is None else acc + d
    return acc


def _rms(x, w):
    ms = jnp.mean(x * x, axis=-1, keepdims=True)
    return x * lax.rsqrt(ms + NORM_EPS) * w


X_HALO = 8


def _in_proj_kernel(x_ref, xp_ref, xn_ref, nw_ref, wqkv_ref, w_ref, wg_ref, cw_ref, ones_ref,
                    qkv_ref, p_ref, g_ref, ext_ref, *, tiles_per_seq):
    jj = pl.program_id(0) % tiles_per_seq
    tm = x_ref.shape[0]
    half = tm // 2
    pad = DN_CONV // 2
    nw = nw_ref[...]
    h_a = _rms(jnp.concatenate([xp_ref[...], x_ref[0:half, :]], axis=0), nw)
    pre_a = _dot(h_a.astype(BF16), wqkv_ref[...])
    h_b = _rms(jnp.concatenate([x_ref[half:, :], xn_ref[...]], axis=0), nw)
    pre_b = _dot(h_b.astype(BF16), wqkv_ref[...])
    ext_ref[0:X_HALO, :] = jnp.where(jj > 0, pre_a[0:X_HALO], 0.0)
    ext_ref[X_HALO:X_HALO + half, :] = pre_a[X_HALO:]
    ext_ref[X_HALO + half:X_HALO + tm, :] = pre_b[0:half]
    ext_ref[X_HALO + tm:, :] = jnp.where(jj < tiles_per_seq - 1, pre_b[half:], 0.0)

    h = jnp.concatenate([h_a[X_HALO:], h_b[0:half]], axis=0).astype(BF16)
    ones = ones_ref[...]

    def conv_silu(c0, c1):
        acc = None
        for k in range(DN_CONV):
            off = X_HALO - pad + k
            term = ext_ref[off:off + tm, c0:c1] * cw_ref[k:k + 1, c0:c1]
            acc = term if acc is None else acc + term
        return acc * jax.nn.sigmoid(acc)

    def l2n(t, scale):
        ss = _sel_dot(t * t, ones, 1)
        return t * (lax.rsqrt(ss + NORM_EPS) * scale)

    def emit_q():
        qkv_ref[:, 0:DN_QK] = l2n(conv_silu(0, DN_QK), DN_DK ** -0.5).astype(BF16)

    def emit_k():
        qkv_ref[:, DN_QK:2 * DN_QK] = l2n(conv_silu(DN_QK, 2 * DN_QK), 1.0).astype(BF16)

    def emit_v():
        qkv_ref[:, 2 * DN_QK:] = conv_silu(2 * DN_QK, DN_CONV_CH).astype(BF16)

    epilogue = [emit_q, emit_k, emit_v]
    n = w_ref.shape[1]
    step = 512
    for idx, c0 in enumerate(range(0, n, step)):
        c1 = min(c0 + step, n)
        p_ref[:, c0:c1] = _dot(h, w_ref[:, c0:c1]).astype(BF16)
        if idx < len(epilogue):
            epilogue[idx]()
    g_ref[...] = _dot(h, wg_ref[...])


def _in_proj(xf, nw, wqkv, w, wg, cw, ones_hd, tm, s):
    m, d = xf.shape
    n = w.shape[1]
    nh = tm // X_HALO
    last = m // X_HALO - 1
    single = pl.Buffered(1)
    return pl.pallas_call(
        functools.partial(_in_proj_kernel, tiles_per_seq=s // tm),
        grid=(m // tm,),
        in_specs=[
            pl.BlockSpec((tm, d), lambda i: (i, 0)),
            pl.BlockSpec((X_HALO, d), lambda i: (jnp.maximum(i * nh - 1, 0), 0)),
            pl.BlockSpec((X_HALO, d), lambda i: (jnp.minimum((i + 1) * nh, last), 0)),
            _const_spec((1, d)),
            pl.BlockSpec((d, DN_CONV_CH), lambda i: (0, 0), pipeline_mode=single),
            pl.BlockSpec((d, n), lambda i: (0, 0), pipeline_mode=single),
            _const_spec((d, GATE_PAD)),
            _const_spec((DN_CONV, DN_CONV_CH)),
            _const_spec((DN_QK, DN_QK)),
        ],
        out_specs=[
            pl.BlockSpec((tm, DN_CONV_CH), lambda i: (i, 0)),
            pl.BlockSpec((tm, n), lambda i: (i, 0)),
            pl.BlockSpec((tm, GATE_PAD), lambda i: (i, 0)),
        ],
        out_shape=[
            jax.ShapeDtypeStruct((m, DN_CONV_CH), BF16),
            jax.ShapeDtypeStruct((m, n), BF16),
            jax.ShapeDtypeStruct((m, GATE_PAD), F32),
        ],
        scratch_shapes=[pltpu.VMEM((tm + 2 * X_HALO, DN_CONV_CH), F32)],
        compiler_params=_cparams(("parallel",)),
        name="in_proj",
    )(xf, xf, xf, nw, wqkv, w, wg, cw, ones_hd)


def _softplus(x):
    return jnp.maximum(x, 0.0) + jnp.log1p(jnp.exp(-jnp.abs(x)))


def _block_diag(x, half_masks):
    xb = x.astype(BF16)
    zeros = jnp.zeros((xb.shape[0], LANES), BF16)
    blocks = []
    for h in range(HEADS_PER_GROUP):
        tile, half = divmod(h, 2)
        kept = xb[:, tile * LANES:(tile + 1) * LANES] * half_masks[half]
        blocks.append(jnp.concatenate([kept, zeros] if tile == 0 else [zeros, kept], axis=1))
    return jnp.concatenate(blocks, axis=0)


def _mask_block_diag(bd, tile_mask):
    c = tile_mask.shape[0]
    zeros = jnp.zeros((c, LANES), BF16)
    blocks = []
    for h in range(HEADS_PER_GROUP):
        tile = h // 2
        kept = bd[h * c:(h + 1) * c, tile * LANES:(tile + 1) * LANES] * tile_mask
        blocks.append(jnp.concatenate([kept, zeros] if tile == 0 else [zeros, kept], axis=1))
    return jnp.concatenate(blocks, axis=0)


def _round_robin(gens):
    live = list(gens)
    while live:
        nxt = []
        for g in live:
            try:
                next(g)
                nxt.append(g)
            except StopIteration:
                pass
        live = nxt


def _delta_kernel(qkvf_ref, qkvb_ref, gf_ref, gb_ref, alog_ref, dtb_ref,
                  tri_ref, esel_ref, bd_ref,
                  of_ref, ob_ref,
                  gc_ref, beta_ref, u_ref, wq_ref, attn_ref, kg_ref, state_ref):
    j = pl.program_id(1)
    bb, ts = qkvf_ref.shape[0], qkvf_ref.shape[1]
    n_chunks = ts // DN_CHUNK
    c = DN_CHUNK

    @pl.when(j == 0)
    def _():
        state_ref[...] = jnp.zeros_like(state_ref)

    lane_g = lax.broadcasted_iota(jnp.int32, (1, GATE_PAD), 1)
    rate = jnp.where(lane_g < 2 * DN_HEADS, jnp.exp(alog_ref[...]), 0.0)
    dtb = dtb_ref[...]

    for bi in range(bb):
        for d, g_ref in ((0, gf_ref), (1, gb_ref)):
            gates = g_ref[bi]
            g = -rate * _softplus(gates + dtb)
            gcd = _sel_dot_left(tri_ref[d], g, 2)
            gc_ref[2 * bi + d] = _sel_dot(gcd, esel_ref[d], 2)
            beta_ref[2 * bi + d] = _sel_dot(jax.nn.sigmoid(gates), esel_ref[2 + d], 1)

    bd_f32 = bd_ref[...].astype(F32)
    lane_t = lax.broadcasted_iota(jnp.int32, (c, LANES), 1)
    half_masks = ((lane_t < DN_DK).astype(BF16), (lane_t >= DN_DK).astype(BF16))
    row = lax.broadcasted_iota(jnp.int32, (c, GROUP_W), 0)
    col = lax.broadcasted_iota(jnp.int32, (c, GROUP_W), 1) % c
    eye = row == col
    eye_f = eye.astype(F32)
    masks = ((row >= col, row > col), (row <= col, row < col))
    base = 8
    same_blk = lambda bs: (row // bs) == (col // bs)
    base_blk = same_blk(base)
    merge_masks = []
    bs = base
    while bs < c:
        off_blk = jnp.logical_and(same_blk(2 * bs), jnp.logical_not(same_blk(bs)))
        merge_masks.append(off_blk[:, :LANES].astype(BF16))
        bs *= 2

    def slot_of(bi, d, hg, chunk):
        return ((2 * bi + d) * N_GROUPS + hg) * n_chunks + chunk

    def prep(qkv_ref, bi, chunk, d, hg):
        rows = pl.ds(pl.multiple_of(chunk * c, c), c)
        l0 = hg * GROUP_W
        q = qkv_ref[bi, rows, l0:l0 + GROUP_W]
        k = qkv_ref[bi, rows, DN_QK + l0:DN_QK + l0 + GROUP_W]
        v = qkv_ref[bi, rows, 2 * DN_QK + l0:2 * DN_QK + l0 + GROUP_W]
        incl, strict = masks[d]
        slot = slot_of(bi, d, hg, chunk)

        bdk = _block_diag(k, half_masks)
        r = _dot_nt(jnp.concatenate([q, k], axis=0), bdk)
        yield
        gcc = gc_ref[2 * bi + d, rows, l0:l0 + GROUP_W]
        beta = beta_ref[2 * bi + d, rows, l0:l0 + GROUP_W]
        gcrow = jnp.sum(jnp.where(eye, gcc, 0.0), axis=0, keepdims=True)
        decay = jnp.where(incl, jnp.exp(jnp.where(incl, gcc - gcrow, 0.0)), 0.0)
        attn_ref[slot] = (r[:c] * decay).astype(BF16)
        nmat = jnp.where(strict, -(r[c:] * beta * decay), 0.0)

        n_base = jnp.where(base_blk, nmat, 0.0)
        p = eye_f + n_base
        n2 = _dot(n_base.astype(BF16), _block_diag(n_base, half_masks))
        yield
        r = _dot(jnp.concatenate([p, n2], axis=0).astype(BF16), _block_diag(n2, half_masks))
        yield
        p = p + r[:c]
        x = _dot(p.astype(BF16), _block_diag(r[c:], half_masks))
        yield
        p = p + x
        bdn = _block_diag(nmat, half_masks)
        for tile_mask in merge_masks:
            pb = p.astype(BF16)
            t = _dot(pb, _mask_block_diag(bdn, tile_mask))
            yield
            x = _dot(t.astype(BF16), _block_diag(pb, half_masks))
            yield
            p = p + x

        kf, vf = k.astype(F32), v.astype(F32)
        egc = jnp.exp(gcc)
        g_end = gcc[c - 1:c, :] if d == 0 else gcc[0:1, :]
        rhs = jnp.concatenate([_block_diag(vf * beta, half_masks),
                               _block_diag(kf * (beta * egc), half_masks)], axis=1)
        uw = _dot(p.astype(BF16), rhs)
        yield
        u_ref[slot] = uw[:, :GROUP_W]
        wq_ref[slot, 0:c, :] = uw[:, GROUP_W:].astype(BF16)
        wq_ref[slot, c:2 * c, :] = (q.astype(F32) * egc).astype(BF16)
        kg_ref[slot] = (kf * jnp.exp(g_end - gcc)).astype(BF16)

    def scan(o_ref, bi, chunk, d, hg):
        r0 = pl.multiple_of(chunk * c, c)
        l0 = hg * GROUP_W
        slot = slot_of(bi, d, hg, chunk)
        sidx = (2 * bi + d) * N_GROUPS + hg
        state = state_ref[sidx]
        r2 = _dot(wq_ref[slot], state.astype(BF16))
        yield
        vnew = u_ref[slot] - r2[:c]
        o = r2[c:] + _dot(attn_ref[slot], _block_diag(vnew, half_masks))
        upd = _dot_tn(kg_ref[slot], vnew.astype(BF16))
        yield
        o_ref[bi, pl.ds(r0, c), l0:l0 + GROUP_W] = o.astype(o_ref.dtype)
        end_row = r0 + (c - 1 if d == 0 else 0)
        etot = jnp.exp(gc_ref[2 * bi + d, pl.ds(end_row, 1), l0:l0 + GROUP_W])
        state_ref[sidx] = state * etot + upd * bd_f32

    def prep_body(bi, carry):
        gens = []
        for chunk in range(n_chunks):
            for hg in range(N_GROUPS):
                gens.append(prep(qkvf_ref, bi, chunk, 0, hg))
                gens.append(prep(qkvb_ref, bi, chunk, 1, hg))
        _round_robin(gens)
        return carry

    lax.fori_loop(0, bb, prep_body, 0)

    def scan_body(ci, carry):
        gens = []
        for bi in range(bb):
            for hg in range(N_GROUPS):
                gens.append(scan(of_ref, bi, ci, 0, hg))
                gens.append(scan(ob_ref, bi, n_chunks - 1 - ci, 1, hg))
        _round_robin(gens)
        return carry

    lax.fori_loop(0, n_chunks, scan_body, 0)


def _delta(qkvn, gates3, alog_row, dtb_row, consts, ts):
    b, s, _ = qkvn.shape
    nj = s // ts
    bb = 4 if b % 4 == 0 else (2 if b % 2 == 0 else 1)
    n_slots = bb * 2 * N_GROUPS * (ts // DN_CHUNK)
    fwd = lambda i, j: (i, j, 0)
    bwd = lambda i, j: (i, nj - 1 - j, 0)
    return pl.pallas_call(
        _delta_kernel,
        grid=(b // bb, nj),
        in_specs=[
            pl.BlockSpec((bb, ts, DN_CONV_CH), fwd),
            pl.BlockSpec((bb, ts, DN_CONV_CH), bwd),
            pl.BlockSpec((bb, ts, GATE_PAD), fwd),
            pl.BlockSpec((bb, ts, GATE_PAD), bwd),
            _const_spec((1, GATE_PAD)),
            _const_spec((1, GATE_PAD)),
            _const_spec((2, ts, ts)),
            _const_spec((4, GATE_PAD, DN_QK)),
            _const_spec((GROUP_W, GROUP_W)),
        ],
        out_specs=[
            pl.BlockSpec((bb, ts, DN_V), fwd),
            pl.BlockSpec((bb, ts, DN_V), bwd),
        ],
        out_shape=[
            jax.ShapeDtypeStruct((b, s, DN_V), BF16),
            jax.ShapeDtypeStruct((b, s, DN_V), BF16),
        ],
        scratch_shapes=[
            pltpu.VMEM((2 * bb, ts, DN_QK), F32),
            pltpu.VMEM((2 * bb, ts, DN_QK), F32),
            pltpu.VMEM((n_slots, DN_CHUNK, GROUP_W), F32),
            pltpu.VMEM((n_slots, 2 * DN_CHUNK, GROUP_W), BF16),
            pltpu.VMEM((n_slots, DN_CHUNK, GROUP_W), BF16),
            pltpu.VMEM((n_slots, DN_CHUNK, GROUP_W), BF16),
            pltpu.VMEM((2 * bb * N_GROUPS, GROUP_W, GROUP_W), F32),
        ],
        compiler_params=_cparams(("parallel", "arbitrary")),
        name="gated_delta",
    )(qkvn, qkvn, gates3, gates3, alog_row, dtb_row,
      consts["tri"], consts["esel"], consts["bd"])


def _attn_kernel(q_ref, kvc_ref, kvp_ref, kvn_ref, sink_ref, o_ref, kv_ref, bias_ref):
    j = pl.program_id(1)
    nj = pl.num_programs(1)
    tq = q_ref.shape[1]
    blk = ATTN_BLOCK
    span = blk + 2 * WINDOW
    n_sub = tq // blk
    n_heads = 2 * ATTN_GROUP

    def head_of(idx):
        return idx // 2 + ATTN_GROUP * (idx % 2)

    @pl.when(j == 0)
    def _():
        qi = lax.broadcasted_iota(jnp.int32, (blk, span), 0)
        kr = lax.broadcasted_iota(jnp.int32, (blk, span), 1)
        dist_i = jnp.abs(qi + WINDOW - kr)
        dist = dist_i.astype(F32)
        for idx in range(n_heads):
            slope = 2.0 ** (-8.0 * (head_of(idx) + 1) / ATTN_HEADS)
            bias_ref[idx] = jnp.where(dist_i <= WINDOW, -slope * dist, -jnp.inf)

    kv_ref[0:WINDOW] = kvp_ref[0]
    kv_ref[WINDOW:WINDOW + tq] = kvc_ref[0]
    kv_ref[WINDOW + tq:] = kvn_ref[0]

    lane = lax.broadcasted_iota(jnp.int32, (1, LANES), 1)
    lo = lane < ATTN_HD
    kcol = lax.broadcasted_iota(jnp.int32, (1, span), 1)
    first_ok = jnp.logical_or(j > 0, kcol >= WINDOW)
    last_ok = jnp.logical_or(j < nj - 1, kcol < WINDOW + blk)
    zero = jnp.zeros((), BF16)
    orow = lax.broadcasted_iota(jnp.int32, (2 * span, LANES), 0)
    olane = lax.broadcasted_iota(jnp.int32, (2 * span, LANES), 1)
    ones_sel = ((orow < span) == (olane < ATTN_HD)).astype(BF16)

    def scores(sb):
        kwin = kv_ref[sb * blk:sb * blk + span, 0:ATTN_KV]
        qs = []
        for g in range(ATTN_GROUP):
            qg = q_ref[0, sb * blk:(sb + 1) * blk, g * LANES:(g + 1) * LANES]
            qs.append(jnp.where(lo, qg, zero))
            qs.append(jnp.where(lo, zero, qg))
        return _dot_nt(jnp.concatenate(qs, axis=0), kwin)

    def finish(sb, s_all):
        vwin = kv_ref[sb * blk:sb * blk + span, ATTN_KV:2 * ATTN_KV]
        es, sink_terms = [], []
        for idx in range(n_heads):
            head = head_of(idx)
            sink = sink_ref[head:head + 1, 0:1]
            sc = s_all[idx * blk:(idx + 1) * blk] + bias_ref[idx]
            if sb == 0:
                sc = jnp.where(first_ok, sc, -jnp.inf)
            if sb == n_sub - 1:
                sc = jnp.where(last_ok, sc, -jnp.inf)
            tiles = [sc[:, t * LANES:(t + 1) * LANES] for t in range(span // LANES)]
            m = jnp.max(functools.reduce(jnp.maximum, tiles), axis=-1, keepdims=True)
            m_b = jnp.broadcast_to(jnp.maximum(m, sink), (blk, LANES))
            es.append(jnp.concatenate([jnp.exp(t - m_b) for t in tiles], axis=1).astype(BF16))
            sink_terms.append(jnp.exp(sink - m_b))
        e_all = jnp.concatenate(
            [jnp.concatenate([es[2 * g], es[2 * g + 1]], axis=1) for g in range(ATTN_GROUP)],
            axis=0)
        v_stack = jnp.concatenate([jnp.where(lo, vwin, zero), jnp.where(lo, zero, vwin)], axis=0)
        nd = _dot(e_all, jnp.concatenate([v_stack, ones_sel], axis=1))
        for g in range(ATTN_GROUP):
            rows = slice(g * blk, (g + 1) * blk)
            den = nd[rows, LANES:] + jnp.where(lo, sink_terms[2 * g], sink_terms[2 * g + 1])
            o_ref[0, sb * blk:(sb + 1) * blk, g * LANES:(g + 1) * LANES] = (
                nd[rows, :LANES] / den).astype(BF16)

    s_next = scores(0)
    for sb in range(n_sub):
        s_cur = s_next
        if sb + 1 < n_sub:
            s_next = scores(sb + 1)
        finish(sb, s_cur)


def _attn(p3, sink_b, q_col, kv_col, tq):
    b, s, _ = p3.shape
    nh = tq // WINDOW
    last = s // WINDOW - 1
    kvw = 2 * ATTN_KV
    span = ATTN_BLOCK + 2 * WINDOW
    return pl.pallas_call(
        _attn_kernel,
        grid=(b, s // tq),
        in_specs=[
            pl.BlockSpec((1, tq, ATTN_Q), lambda i, j: (i, j, q_col // ATTN_Q)),
            pl.BlockSpec((1, tq, kvw), lambda i, j: (i, j, kv_col // kvw)),
            pl.BlockSpec((1, WINDOW, kvw), lambda i, j: (i, jnp.maximum(j * nh - 1, 0), kv_col // kvw)),
            pl.BlockSpec((1, WINDOW, kvw), lambda i, j: (i, jnp.minimum((j + 1) * nh, last), kv_col // kvw)),
            _const_spec((ATTN_HEADS, LANES)),
        ],
        out_specs=pl.BlockSpec((1, tq, ATTN_Q), lambda i, j: (i, j, 0)),
        out_shape=jax.ShapeDtypeStruct((b, s, ATTN_Q), BF16),
        scratch_shapes=[
            pltpu.VMEM((tq + 2 * WINDOW, kvw), BF16),
            pltpu.VMEM((ATTN_HEADS, ATTN_BLOCK, span), F32),
        ],
        compiler_params=_cparams(("parallel", "arbitrary")),
        name="window_attn",
    )(p3, p3, p3, p3, sink_b)


def _mix_mlp_kernel(of_ref, ob_ref, z_ref, oatt_ref, bg_ref, x_ref,
                    dnw_ref, ones_ref, wa_ref, wb_ref, wo_ref, nmix_ref,
                    npre_ref, w1_ref, w2_ref, npost_ref, out_ref):
    d_model = x_ref.shape[1]
    oa = of_ref[...].astype(F32) + ob_ref[...].astype(F32)
    ms = _sel_dot(oa * oa, ones_ref[...], 1) * (1.0 / DN_DV)
    z = z_ref[...].astype(F32)
    on = oa * lax.rsqrt(ms + NORM_EPS) * dnw_ref[...] * (z * jax.nn.sigmoid(z))
    ya = _dot(on.astype(BF16), wa_ref[...])
    yb = _dot(oatt_ref[...], wb_ref[...])
    ga = jax.nn.sigmoid(bg_ref[:, 0:d_model].astype(F32))
    gb = jax.nn.sigmoid(bg_ref[:, d_model:2 * d_model].astype(F32))
    mix = _dot((ga * ya + gb * yb).astype(BF16), wo_ref[...])
    x1 = x_ref[...] + _rms(mix, nmix_ref[...])

    h = _rms(x1, npre_ref[...]).astype(BF16)
    d_ff = w1_ref.shape[1]
    step = 1024
    acc = None
    for c0 in range(0, d_ff, step):
        u = jnp.maximum(_dot(h, w1_ref[:, c0:c0 + step]), 0.0)
        t = _dot((u * u).astype(BF16), w2_ref[c0:c0 + step, :])
        acc = t if acc is None else acc + t
    out_ref[...] = x1 + _rms(acc, npost_ref[...])


def _mix_mlp(o_f, o_b, p, o_att, xf, dnw, ones_hd, wa, wb, wo, nmix, npre, w1, w2, npost, cols, tm):
    m, d = xf.shape
    d_ff = w1.shape[1]
    row = lambda i: (i, 0)
    single = pl.Buffered(1)
    resident = lambda shape: pl.BlockSpec(shape, lambda i: (0, 0), pipeline_mode=single)
    return pl.pallas_call(
        _mix_mlp_kernel,
        grid=(m // tm,),
        in_specs=[
            pl.BlockSpec((tm, DN_V), row),
            pl.BlockSpec((tm, DN_V), row),
            pl.BlockSpec((tm, DN_V), lambda i: (i, cols["z"] // DN_V)),
            pl.BlockSpec((tm, ATTN_Q), row),
            pl.BlockSpec((tm, 2 * d), lambda i: (i, cols["bg"] // (2 * d))),
            pl.BlockSpec((tm, d), row),
            _const_spec((1, DN_V)),
            _const_spec((DN_V, DN_V)),
            resident((DN_V, d)),
            resident((ATTN_Q, d)),
            resident((d, d)),
            _const_spec((1, d)),
            _const_spec((1, d)),
            resident((d, d_ff)),
            resident((d_ff, d)),
            _const_spec((1, d)),
        ],
        out_specs=pl.BlockSpec((tm, d), row),
        out_shape=jax.ShapeDtypeStruct((m, d), F32),
        compiler_params=_cparams(("parallel",)),
        name="mix_mlp",
    )(o_f, o_b, p, o_att, p, xf, dnw, ones_hd, wa, wb, wo, nmix, npre, w1, w2, npost)


def _constants(ts):
    idx = jnp.arange(ts)
    same_chunk = (idx[:, None] // DN_CHUNK) == (idx[None, :] // DN_CHUNK)
    tri = jnp.stack([jnp.logical_and(same_chunk, idx[:, None] >= idx[None, :]),
                     jnp.logical_and(same_chunk, idx[:, None] <= idx[None, :])]).astype(BF16)
    lanes = jnp.arange(DN_QK) // DN_DK
    gate_rows = jnp.arange(GATE_PAD)
    esel = jnp.stack([(gate_rows[:, None] == (off + lanes)[None, :]).astype(BF16)
                      for off in (0, DN_HEADS, 2 * DN_HEADS, 3 * DN_HEADS)])
    gi = jnp.arange(GROUP_W) // DN_DK
    bd = (gi[:, None] == gi[None, :]).astype(BF16)
    hi = jnp.arange(DN_QK) // DN_DK
    ones_hd = (hi[:, None] == hi[None, :]).astype(BF16)
    return {"tri": tri, "esel": esel, "bd": bd, "ones_hd": ones_hd}


def _pick(total, want):
    t = min(want, total)
    while total % t:
        t //= 2
    return t


def kernel(x, w_in, conv_w, a_log, dt_bias, dn_norm_w, attn_sink, w_up_a, w_up_b, w_out,
           norm_mix_pre, norm_mix_post, norm_mlp_pre, norm_mlp_post, w_mlp_in, w_mlp_out):
    b, s, d = x.shape
    depth = w_in.shape[0]
    m = b * s
    tm = _pick(s, 512)
    ts_delta = _pick(s, 256)
    tq = _pick(s, 512)
    consts = _constants(ts_delta)

    o_dz = DN_CONV_CH
    o_g = o_dz + DN_V
    o_aq = o_g + N_GATES
    o_ak = o_aq + ATTN_Q
    o_bg = o_ak + 2 * ATTN_KV
    cols = {"bg": 0, "z": 2 * d, "aq": 2 * d + DN_V, "akv": 2 * d + DN_V + ATTN_Q}
    head_order = [h for g in range(ATTN_GROUP) for h in (g, g + ATTN_GROUP)]
    row = lambda v: v.reshape(1, -1).astype(F32)

    xf = x.reshape(m, d)
    for l in range(depth):
        wl = w_in[l]
        aq_cols = [wl[:, o_aq + h * ATTN_HD:o_aq + (h + 1) * ATTN_HD] * (ATTN_HD ** -0.5)
                   for h in head_order]
        w_main = jnp.concatenate(
            [wl[:, o_bg:o_bg + 2 * d], wl[:, o_dz:o_g]] + aq_cols + [wl[:, o_ak:o_ak + 2 * ATTN_KV]],
            axis=1).astype(BF16)
        w_qkv = wl[:, 0:DN_CONV_CH].astype(BF16)
        w_gate = jnp.pad(wl[:, o_g:o_g + N_GATES], ((0, 0), (0, GATE_PAD - N_GATES))).astype(BF16)
        wb_rows = jnp.concatenate(
            [w_up_b[l][h * ATTN_HD:(h + 1) * ATTN_HD] for h in head_order], axis=0).astype(BF16)

        qkvn, p, gates = _in_proj(xf, row(norm_mix_pre[l]), w_qkv, w_main, w_gate,
                                  conv_w[l].astype(F32), consts["ones_hd"], tm, s)

        alog_row = jnp.pad(a_log[l].reshape(1, -1).astype(F32), ((0, 0), (0, GATE_PAD - 2 * DN_HEADS)))
        dtb_row = jnp.pad(dt_bias[l].reshape(1, -1).astype(F32), ((0, 0), (0, GATE_PAD - 2 * DN_HEADS)))
        o_f, o_b = _delta(qkvn.reshape(b, s, DN_CONV_CH), gates.reshape(b, s, GATE_PAD),
                          alog_row, dtb_row, consts, ts_delta)

        sink_b = jnp.broadcast_to(attn_sink[l].astype(F32)[:, None], (ATTN_HEADS, LANES))
        o_att = _attn(p.reshape(b, s, p.shape[1]), sink_b, cols["aq"], cols["akv"], tq)

        dnw = jnp.tile(dn_norm_w[l].astype(F32), DN_HEADS).reshape(1, DN_V)
        xf = _mix_mlp(o_f.reshape(m, DN_V), o_b.reshape(m, DN_V), p, o_att.reshape(m, ATTN_Q), xf,
                      dnw, consts["ones_hd"], w_up_a[l].astype(BF16), wb_rows, w_out[l].astype(BF16),
                      row(norm_mix_post[l]), row(norm_mlp_pre[l]), w_mlp_in[l].astype(BF16),
                      w_mlp_out[l].astype(BF16), row(norm_mlp_post[l]), cols, tm)
    return xf.reshape(b, s, d)
```

```python
import functools

import jax
import jax.numpy as jnp
from jax import lax
from jax.experimental import pallas as pl
from jax.experimental.pallas import tpu as pltpu

F32 = jnp.float32
BF16 = jnp.bfloat16

NORM_EPS = 1e-6
DN_HEADS = 8
DN_DK = 64
DN_DV = 64
DN_CONV = 5
DN_CHUNK = 64
ATTN_HEADS = 8
ATTN_KV_HEADS = 2
ATTN_GROUP = ATTN_HEADS // ATTN_KV_HEADS
ATTN_HD = 64
WINDOW = 128
ATTN_BLOCK = 128

DN_QK = DN_HEADS * DN_DK
DN_V = DN_HEADS * DN_DV
DN_CONV_CH = 2 * DN_QK + DN_V
ATTN_Q = ATTN_HEADS * ATTN_HD
ATTN_KV = ATTN_KV_HEADS * ATTN_HD
N_GATES = 4 * DN_HEADS

LANES = 128
GATE_PAD = LANES
HEADS_PER_GROUP = 4
GROUP_W = HEADS_PER_GROUP * DN_DK
N_GROUPS = DN_HEADS // HEADS_PER_GROUP

VMEM_LIMIT = 56 * 1024 * 1024


def _cparams(sem):
    return pltpu.CompilerParams(dimension_semantics=sem, vmem_limit_bytes=VMEM_LIMIT)


def _const_spec(shape):
    nd = len(shape)
    return pl.BlockSpec(shape, lambda *_: (0,) * nd)


def _layer_spec(shape, layer, single=False):
    nd = len(shape)
    kw = {"pipeline_mode": pl.Buffered(1)} if single else {}
    return pl.BlockSpec((None,) + tuple(shape), lambda *_: (layer,) + (0,) * nd, **kw)


def _dot(a, b):
    return jnp.dot(a, b, preferred_element_type=F32)


def _dot_nt(a, b):
    return lax.dot_general(a, b, (((1,), (1,)), ((), ())), preferred_element_type=F32)


def _dot_tn(a, b):
    return lax.dot_general(a, b, (((0,), (0,)), ((), ())), preferred_element_type=F32)


def _split_terms(x, terms):
    out, r = [], x
    for t in range(terms):
        hi = r.astype(BF16)
        out.append(hi)
        if t + 1 < terms:
            r = r - hi.astype(F32)
    return out


def _sel_dot(x, sel, terms):
    acc = None
    for piece in _split_terms(x, terms):
        d = _dot(piece, sel)
        acc = d if acc is None else acc + d
    return acc


def _sel_dot_left(sel, x, terms):
    acc = None
    for piece in _split_terms(x, terms):
        d = _dot(sel, piece)
        acc = d if acc is None else acc + d
    return acc


def _rms(x, w):
    ms = jnp.mean(x * x, axis=-1, keepdims=True)
    return x * lax.rsqrt(ms + NORM_EPS) * w


X_HALO = 8


def _in_proj_kernel(x_ref, xp_ref, xn_ref, nw_ref, wqkv_ref, w_ref, wg_ref, cw_ref, ones_ref,
                    qkv_ref, p_ref, g_ref, ext_ref, *, tiles_per_seq):
    jj = pl.program_id(0) % tiles_per_seq
    tm = x_ref.shape[0]
    half = tm // 2
    pad = DN_CONV // 2
    nw = nw_ref[...]
    h_a = _rms(jnp.concatenate([xp_ref[...], x_ref[0:half, :]], axis=0), nw)
    pre_a = _dot(h_a.astype(BF16), wqkv_ref[...])
    h_b = _rms(jnp.concatenate([x_ref[half:, :], xn_ref[...]], axis=0), nw)
    pre_b = _dot(h_b.astype(BF16), wqkv_ref[...])
    ext_ref[0:X_HALO, :] = jnp.where(jj > 0, pre_a[0:X_HALO], 0.0)
    ext_ref[X_HALO:X_HALO + half, :] = pre_a[X_HALO:]
    ext_ref[X_HALO + half:X_HALO + tm, :] = pre_b[0:half]
    ext_ref[X_HALO + tm:, :] = jnp.where(jj < tiles_per_seq - 1, pre_b[half:], 0.0)

    h = jnp.concatenate([h_a[X_HALO:], h_b[0:half]], axis=0).astype(BF16)
    ones = ones_ref[...]

    def conv_silu(c0, c1):
        acc = None
        for k in range(DN_CONV):
            off = X_HALO - pad + k
            term = ext_ref[off:off + tm, c0:c1] * cw_ref[k:k + 1, c0:c1]
            acc = term if acc is None else acc + term
        return acc * jax.nn.sigmoid(acc)

    def l2n(t, scale):
        ss = _sel_dot(t * t, ones, 1)
        return t * (lax.rsqrt(ss + NORM_EPS) * scale)

    def emit_q():
        qkv_ref[:, 0:DN_QK] = l2n(conv_silu(0, DN_QK), DN_DK ** -0.5).astype(BF16)

    def emit_k():
        qkv_ref[:, DN_QK:2 * DN_QK] = l2n(conv_silu(DN_QK, 2 * DN_QK), 1.0).astype(BF16)

    def emit_v():
        qkv_ref[:, 2 * DN_QK:] = conv_silu(2 * DN_QK, DN_CONV_CH).astype(BF16)

    epilogue = [emit_q, emit_k, emit_v]
    n = w_ref.shape[1]
    step = 512
    for idx, c0 in enumerate(range(0, n, step)):
        c1 = min(c0 + step, n)
        p_ref[:, c0:c1] = _dot(h, w_ref[:, c0:c1]).astype(BF16)
        if idx < len(epilogue):
            epilogue[idx]()
    g_ref[...] = _dot(h, wg_ref[...])


def _in_proj(xf, nw, wqkv, w, wg, cw, ones_hd, tm, s, layer):
    m, d = xf.shape
    n = w.shape[2]
    nh = tm // X_HALO
    last = m // X_HALO - 1
    return pl.pallas_call(
        functools.partial(_in_proj_kernel, tiles_per_seq=s // tm),
        grid=(m // tm,),
        in_specs=[
            pl.BlockSpec((tm, d), lambda i: (i, 0)),
            pl.BlockSpec((X_HALO, d), lambda i: (jnp.maximum(i * nh - 1, 0), 0)),
            pl.BlockSpec((X_HALO, d), lambda i: (jnp.minimum((i + 1) * nh, last), 0)),
            _layer_spec((1, d), layer),
            _layer_spec((d, DN_CONV_CH), layer, single=True),
            _layer_spec((d, n), layer, single=True),
            _layer_spec((d, GATE_PAD), layer),
            _layer_spec((DN_CONV, DN_CONV_CH), layer),
            _const_spec((DN_QK, DN_QK)),
        ],
        out_specs=[
            pl.BlockSpec((tm, DN_CONV_CH), lambda i: (i, 0)),
            pl.BlockSpec((tm, n), lambda i: (i, 0)),
            pl.BlockSpec((tm, GATE_PAD), lambda i: (i, 0)),
        ],
        out_shape=[
            jax.ShapeDtypeStruct((m, DN_CONV_CH), BF16),
            jax.ShapeDtypeStruct((m, n), BF16),
            jax.ShapeDtypeStruct((m, GATE_PAD), F32),
        ],
        scratch_shapes=[pltpu.VMEM((tm + 2 * X_HALO, DN_CONV_CH), F32)],
        compiler_params=_cparams(("parallel",)),
        name="in_proj",
    )(xf, xf, xf, nw, wqkv, w, wg, cw, ones_hd)


def _softplus(x):
    return jnp.maximum(x, 0.0) + jnp.log1p(jnp.exp(-jnp.abs(x)))


def _block_diag(x, half_masks):
    xb = x.astype(BF16)
    zeros = jnp.zeros((xb.shape[0], LANES), BF16)
    blocks = []
    for h in range(HEADS_PER_GROUP):
        tile, half = divmod(h, 2)
        kept = xb[:, tile * LANES:(tile + 1) * LANES] * half_masks[half]
        blocks.append(jnp.concatenate([kept, zeros] if tile == 0 else [zeros, kept], axis=1))
    return jnp.concatenate(blocks, axis=0)


def _mask_block_diag(bd, tile_mask):
    c = tile_mask.shape[0]
    zeros = jnp.zeros((c, LANES), BF16)
    blocks = []
    for h in range(HEADS_PER_GROUP):
        tile = h // 2
        kept = bd[h * c:(h + 1) * c, tile * LANES:(tile + 1) * LANES] * tile_mask
        blocks.append(jnp.concatenate([kept, zeros] if tile == 0 else [zeros, kept], axis=1))
    return jnp.concatenate(blocks, axis=0)


def _round_robin(gens):
    live = list(gens)
    while live:
        nxt = []
        for g in live:
            try:
                next(g)
                nxt.append(g)
            except StopIteration:
                pass
        live = nxt


def _delta_kernel(qkvf_ref, qkvb_ref, gf_ref, gb_ref, alog_ref, dtb_ref,
                  tri_ref, esel_ref, bd_ref,
                  of_ref, ob_ref,
                  gc_ref, beta_ref, u_ref, wq_ref, attn_ref, kg_ref, state_ref):
    j = pl.program_id(1)
    bb, ts = qkvf_ref.shape[0], qkvf_ref.shape[1]
    n_chunks = ts // DN_CHUNK
    c = DN_CHUNK

    @pl.when(j == 0)
    def _():
        state_ref[...] = jnp.zeros_like(state_ref)

    lane_g = lax.broadcasted_iota(jnp.int32, (1, GATE_PAD), 1)
    rate = jnp.where(lane_g < 2 * DN_HEADS, jnp.exp(alog_ref[...]), 0.0)
    dtb = dtb_ref[...]

    for bi in range(bb):
        for d, g_ref in ((0, gf_ref), (1, gb_ref)):
            gates = g_ref[bi]
            g = -rate * _softplus(gates + dtb)
            gcd = _sel_dot_left(tri_ref[d], g, 2)
            gc_ref[2 * bi + d] = _sel_dot(gcd, esel_ref[d], 2)
            beta_ref[2 * bi + d] = _sel_dot(jax.nn.sigmoid(gates), esel_ref[2 + d], 1)

    bd_f32 = bd_ref[...].astype(F32)
    lane_t = lax.broadcasted_iota(jnp.int32, (c, LANES), 1)
    half_masks = ((lane_t < DN_DK).astype(BF16), (lane_t >= DN_DK).astype(BF16))
    row = lax.broadcasted_iota(jnp.int32, (c, GROUP_W), 0)
    col = lax.broadcasted_iota(jnp.int32, (c, GROUP_W), 1) % c
    eye = row == col
    eye_f = eye.astype(F32)
    masks = ((row >= col, row > col), (row <= col, row < col))
    base = 8
    same_blk = lambda bs: (row // bs) == (col // bs)
    base_blk = same_blk(base)
    merge_masks = []
    bs = base
    while bs < c:
        off_blk = jnp.logical_and(same_blk(2 * bs), jnp.logical_not(same_blk(bs)))
        merge_masks.append(off_blk[:, :LANES].astype(BF16))
        bs *= 2

    def slot_of(bi, d, hg, chunk):
        return ((2 * bi + d) * N_GROUPS + hg) * n_chunks + chunk

    def prep(qkv_ref, bi, chunk, d, hg):
        rows = pl.ds(pl.multiple_of(chunk * c, c), c)
        l0 = hg * GROUP_W
        q = qkv_ref[bi, rows, l0:l0 + GROUP_W]
        k = qkv_ref[bi, rows, DN_QK + l0:DN_QK + l0 + GROUP_W]
        v = qkv_ref[bi, rows, 2 * DN_QK + l0:2 * DN_QK + l0 + GROUP_W]
        incl, strict = masks[d]
        slot = slot_of(bi, d, hg, chunk)

        bdk = _block_diag(k, half_masks)
        r = _dot_nt(jnp.concatenate([q, k], axis=0), bdk)
        yield
        gcc = gc_ref[2 * bi + d, rows, l0:l0 + GROUP_W]
        beta = beta_ref[2 * bi + d, rows, l0:l0 + GROUP_W]
        gcrow = jnp.sum(jnp.where(eye, gcc, 0.0), axis=0, keepdims=True)
        decay = jnp.where(incl, jnp.exp(jnp.where(incl, gcc - gcrow, 0.0)), 0.0)
        attn_ref[slot] = (r[:c] * decay).astype(BF16)
        nmat = jnp.where(strict, -(r[c:] * beta * decay), 0.0)

        n_base = jnp.where(base_blk, nmat, 0.0)
        p = eye_f + n_base
        n2 = _dot(n_base.astype(BF16), _block_diag(n_base, half_masks))
        yield
        r = _dot(jnp.concatenate([p, n2], axis=0).astype(BF16), _block_diag(n2, half_masks))
        yield
        p = p + r[:c]
        x = _dot(p.astype(BF16), _block_diag(r[c:], half_masks))
        yield
        p = p + x
        bdn = _block_diag(nmat, half_masks)
        for tile_mask in merge_masks:
            pb = p.astype(BF16)
            t = _dot(pb, _mask_block_diag(bdn, tile_mask))
            yield
            x = _dot(t.astype(BF16), _block_diag(pb, half_masks))
            yield
            p = p + x

        kf, vf = k.astype(F32), v.astype(F32)
        egc = jnp.exp(gcc)
        g_end = gcc[c - 1:c, :] if d == 0 else gcc[0:1, :]
        rhs = jnp.concatenate([_block_diag(vf * beta, half_masks),
                               _block_diag(kf * (beta * egc), half_masks)], axis=1)
        uw = _dot(p.astype(BF16), rhs)
        yield
        u_ref[slot] = uw[:, :GROUP_W]
        wq_ref[slot, 0:c, :] = uw[:, GROUP_W:].astype(BF16)
        wq_ref[slot, c:2 * c, :] = (q.astype(F32) * egc).astype(BF16)
        kg_ref[slot] = (kf * jnp.exp(g_end - gcc)).astype(BF16)

    def scan(o_ref, bi, chunk, d, hg):
        r0 = pl.multiple_of(chunk * c, c)
        l0 = hg * GROUP_W
        slot = slot_of(bi, d, hg, chunk)
        sidx = (2 * bi + d) * N_GROUPS + hg
        state = state_ref[sidx]
        r2 = _dot(wq_ref[slot], state.astype(BF16))
        yield
        vnew = u_ref[slot] - r2[:c]
        o = r2[c:] + _dot(attn_ref[slot], _block_diag(vnew, half_masks))
        upd = _dot_tn(kg_ref[slot], vnew.astype(BF16))
        yield
        o_ref[bi, pl.ds(r0, c), l0:l0 + GROUP_W] = o.astype(o_ref.dtype)
        end_row = r0 + (c - 1 if d == 0 else 0)
        etot = jnp.exp(gc_ref[2 * bi + d, pl.ds(end_row, 1), l0:l0 + GROUP_W])
        state_ref[sidx] = state * etot + upd * bd_f32

    def prep_body(bi, carry):
        gens = []
        for chunk in range(n_chunks):
            for hg in range(N_GROUPS):
                gens.append(prep(qkvf_ref, bi, chunk, 0, hg))
                gens.append(prep(qkvb_ref, bi, chunk, 1, hg))
        _round_robin(gens)
        return carry

    lax.fori_loop(0, bb, prep_body, 0)

    def scan_body(ci, carry):
        gens = []
        for bi in range(bb):
            for hg in range(N_GROUPS):
                gens.append(scan(of_ref, bi, ci, 0, hg))
                gens.append(scan(ob_ref, bi, n_chunks - 1 - ci, 1, hg))
        _round_robin(gens)
        return carry

    lax.fori_loop(0, n_chunks, scan_body, 0)


def _delta(qkvn, gates3, alog_row, dtb_row, consts, ts, layer):
    b, s, _ = qkvn.shape
    nj = s // ts
    bb = 4 if b % 4 == 0 else (2 if b % 2 == 0 else 1)
    n_slots = bb * 2 * N_GROUPS * (ts // DN_CHUNK)
    fwd = lambda i, j: (i, j, 0)
    bwd = lambda i, j: (i, nj - 1 - j, 0)
    return pl.pallas_call(
        _delta_kernel,
        grid=(b // bb, nj),
        in_specs=[
            pl.BlockSpec((bb, ts, DN_CONV_CH), fwd),
            pl.BlockSpec((bb, ts, DN_CONV_CH), bwd),
            pl.BlockSpec((bb, ts, GATE_PAD), fwd),
            pl.BlockSpec((bb, ts, GATE_PAD), bwd),
            _layer_spec((1, GATE_PAD), layer),
            _layer_spec((1, GATE_PAD), layer),
            _const_spec((2, ts, ts)),
            _const_spec((4, GATE_PAD, DN_QK)),
            _const_spec((GROUP_W, GROUP_W)),
        ],
        out_specs=[
            pl.BlockSpec((bb, ts, DN_V), fwd),
            pl.BlockSpec((bb, ts, DN_V), bwd),
        ],
        out_shape=[
            jax.ShapeDtypeStruct((b, s, DN_V), BF16),
            jax.ShapeDtypeStruct((b, s, DN_V), BF16),
        ],
        scratch_shapes=[
            pltpu.VMEM((2 * bb, ts, DN_QK), F32),
            pltpu.VMEM((2 * bb, ts, DN_QK), F32),
            pltpu.VMEM((n_slots, DN_CHUNK, GROUP_W), F32),
            pltpu.VMEM((n_slots, 2 * DN_CHUNK, GROUP_W), BF16),
            pltpu.VMEM((n_slots, DN_CHUNK, GROUP_W), BF16),
            pltpu.VMEM((n_slots, DN_CHUNK, GROUP_W), BF16),
            pltpu.VMEM((2 * bb * N_GROUPS, GROUP_W, GROUP_W), F32),
        ],
        compiler_params=_cparams(("parallel", "arbitrary")),
        name="gated_delta",
    )(qkvn, qkvn, gates3, gates3, alog_row, dtb_row,
      consts["tri"], consts["esel"], consts["bd"])


def _attn_kernel(q_ref, kvc_ref, kvp_ref, kvn_ref, sink_ref, o_ref, kv_ref, bias_ref):
    j = pl.program_id(1)
    nj = pl.num_programs(1)
    tq = q_ref.shape[1]
    blk = ATTN_BLOCK
    span = blk + 2 * WINDOW
    n_sub = tq // blk
    n_heads = 2 * ATTN_GROUP

    def head_of(idx):
        return idx // 2 + ATTN_GROUP * (idx % 2)

    @pl.when(j == 0)
    def _():
        qi = lax.broadcasted_iota(jnp.int32, (blk, span), 0)
        kr = lax.broadcasted_iota(jnp.int32, (blk, span), 1)
        dist_i = jnp.abs(qi + WINDOW - kr)
        dist = dist_i.astype(F32)
        for idx in range(n_heads):
            slope = 2.0 ** (-8.0 * (head_of(idx) + 1) / ATTN_HEADS)
            bias_ref[idx] = jnp.where(dist_i <= WINDOW, -slope * dist, -jnp.inf)

    kv_ref[0:WINDOW] = kvp_ref[0]
    kv_ref[WINDOW:WINDOW + tq] = kvc_ref[0]
    kv_ref[WINDOW + tq:] = kvn_ref[0]

    lane = lax.broadcasted_iota(jnp.int32, (1, LANES), 1)
    lo = lane < ATTN_HD
    kcol = lax.broadcasted_iota(jnp.int32, (1, span), 1)
    first_ok = jnp.logical_or(j > 0, kcol >= WINDOW)
    last_ok = jnp.logical_or(j < nj - 1, kcol < WINDOW + blk)
    zero = jnp.zeros((), BF16)
    orow = lax.broadcasted_iota(jnp.int32, (2 * span, LANES), 0)
    olane = lax.broadcasted_iota(jnp.int32, (2 * span, LANES), 1)
    ones_sel = ((orow < span) == (olane < ATTN_HD)).astype(BF16)

    def scores(sb):
        kwin = kv_ref[sb * blk:sb * blk + span, 0:ATTN_KV]
        qs = []
        for g in range(ATTN_GROUP):
            qg = q_ref[0, sb * blk:(sb + 1) * blk, g * LANES:(g + 1) * LANES]
            qs.append(jnp.where(lo, qg, zero))
            qs.append(jnp.where(lo, zero, qg))
        return _dot_nt(jnp.concatenate(qs, axis=0), kwin)

    def finish(sb, s_all):
        vwin = kv_ref[sb * blk:sb * blk + span, ATTN_KV:2 * ATTN_KV]
        es, sink_terms = [], []
        for idx in range(n_heads):
            head = head_of(idx)
            sink = sink_ref[head:head + 1, 0:1]
            sc = s_all[idx * blk:(idx + 1) * blk] + bias_ref[idx]
            if sb == 0:
                sc = jnp.where(first_ok, sc, -jnp.inf)
            if sb == n_sub - 1:
                sc = jnp.where(last_ok, sc, -jnp.inf)
            tiles = [sc[:, t * LANES:(t + 1) * LANES] for t in range(span // LANES)]
            m = jnp.max(functools.reduce(jnp.maximum, tiles), axis=-1, keepdims=True)
            m_b = jnp.broadcast_to(jnp.maximum(m, sink), (blk, LANES))
            es.append(jnp.concatenate([jnp.exp(t - m_b) for t in tiles], axis=1).astype(BF16))
            sink_terms.append(jnp.exp(sink - m_b))
        e_all = jnp.concatenate(
            [jnp.concatenate([es[2 * g], es[2 * g + 1]], axis=1) for g in range(ATTN_GROUP)],
            axis=0)
        v_stack = jnp.concatenate([jnp.where(lo, vwin, zero), jnp.where(lo, zero, vwin)], axis=0)
        nd = _dot(e_all, jnp.concatenate([v_stack, ones_sel], axis=1))
        for g in range(ATTN_GROUP):
            rows = slice(g * blk, (g + 1) * blk)
            den = nd[rows, LANES:] + jnp.where(lo, sink_terms[2 * g], sink_terms[2 * g + 1])
            o_ref[0, sb * blk:(sb + 1) * blk, g * LANES:(g + 1) * LANES] = (
                nd[rows, :LANES] / den).astype(BF16)

    for sb in range(n_sub):
        finish(sb, scores(sb))


def _attn(p3, sink_b, q_col, kv_col, tq, layer):
    b, s, _ = p3.shape
    nh = tq // WINDOW
    last = s // WINDOW - 1
    kvw = 2 * ATTN_KV
    span = ATTN_BLOCK + 2 * WINDOW
    return pl.pallas_call(
        _attn_kernel,
        grid=(b, s // tq),
        in_specs=[
            pl.BlockSpec((1, tq, ATTN_Q), lambda i, j: (i, j, q_col // ATTN_Q)),
            pl.BlockSpec((1, tq, kvw), lambda i, j: (i, j, kv_col // kvw)),
            pl.BlockSpec((1, WINDOW, kvw), lambda i, j: (i, jnp.maximum(j * nh - 1, 0), kv_col // kvw)),
            pl.BlockSpec((1, WINDOW, kvw), lambda i, j: (i, jnp.minimum((j + 1) * nh, last), kv_col // kvw)),
            _layer_spec((ATTN_HEADS, LANES), layer),
        ],
        out_specs=pl.BlockSpec((1, tq, ATTN_Q), lambda i, j: (i, j, 0)),
        out_shape=jax.ShapeDtypeStruct((b, s, ATTN_Q), BF16),
        scratch_shapes=[
            pltpu.VMEM((tq + 2 * WINDOW, kvw), BF16),
            pltpu.VMEM((ATTN_HEADS, ATTN_BLOCK, span), F32),
        ],
        compiler_params=_cparams(("parallel", "arbitrary")),
        name="window_attn",
    )(p3, p3, p3, p3, sink_b)


def _mix_mlp_kernel(of_ref, ob_ref, z_ref, oatt_ref, bg_ref, x_ref,
                    dnw_ref, ones_ref, wa_ref, wb_ref, wo_ref, nmix_ref,
                    npre_ref, w1_ref, w2_ref, npost_ref, out_ref):
    d_model = x_ref.shape[1]
    oa = of_ref[...].astype(F32) + ob_ref[...].astype(F32)
    ms = _sel_dot(oa * oa, ones_ref[...], 1) * (1.0 / DN_DV)
    z = z_ref[...].astype(F32)
    on = oa * lax.rsqrt(ms + NORM_EPS) * dnw_ref[...] * (z * jax.nn.sigmoid(z))
    ya = _dot(on.astype(BF16), wa_ref[...])
    yb = _dot(oatt_ref[...], wb_ref[...])
    ga = jax.nn.sigmoid(bg_ref[:, 0:d_model].astype(F32))
    gb = jax.nn.sigmoid(bg_ref[:, d_model:2 * d_model].astype(F32))
    mix = _dot((ga * ya + gb * yb).astype(BF16), wo_ref[...])
    x1 = x_ref[...] + _rms(mix, nmix_ref[...])

    h = _rms(x1, npre_ref[...]).astype(BF16)
    d_ff = w1_ref.shape[1]
    step = 1024
    acc = None
    for c0 in range(0, d_ff, step):
        u = jnp.maximum(_dot(h, w1_ref[:, c0:c0 + step]), 0.0)
        t = _dot((u * u).astype(BF16), w2_ref[c0:c0 + step, :])
        acc = t if acc is None else acc + t
    out_ref[...] = x1 + _rms(acc, npost_ref[...])


def _mix_mlp(o_f, o_b, p, o_att, xf, dnw, ones_hd, wa, wb, wo, nmix, npre, w1, w2, npost, cols, tm,
             layer):
    m, d = xf.shape
    d_ff = w1.shape[2]
    row = lambda i: (i, 0)
    resident = lambda shape: _layer_spec(shape, layer, single=True)
    vec = lambda width: _layer_spec((1, width), layer)
    return pl.pallas_call(
        _mix_mlp_kernel,
        grid=(m // tm,),
        in_specs=[
            pl.BlockSpec((tm, DN_V), row),
            pl.BlockSpec((tm, DN_V), row),
            pl.BlockSpec((tm, DN_V), lambda i: (i, cols["z"] // DN_V)),
            pl.BlockSpec((tm, ATTN_Q), row),
            pl.BlockSpec((tm, 2 * d), lambda i: (i, cols["bg"] // (2 * d))),
            pl.BlockSpec((tm, d), row),
            vec(DN_V),
            _const_spec((DN_V, DN_V)),
            resident((DN_V, d)),
            resident((ATTN_Q, d)),
            resident((d, d)),
            vec(d),
            vec(d),
            resident((d, d_ff)),
            resident((d_ff, d)),
            vec(d),
        ],
        out_specs=pl.BlockSpec((tm, d), row),
        out_shape=jax.ShapeDtypeStruct((m, d), F32),
        compiler_params=_cparams(("parallel",)),
        name="mix_mlp",
    )(o_f, o_b, p, o_att, p, xf, dnw, ones_hd, wa, wb, wo, nmix, npre, w1, w2, npost)


def _constants(ts):
    idx = jnp.arange(ts)
    same_chunk = (idx[:, None] // DN_CHUNK) == (idx[None, :] // DN_CHUNK)
    tri = jnp.stack([jnp.logical_and(same_chunk, idx[:, None] >= idx[None, :]),
                     jnp.logical_and(same_chunk, idx[:, None] <= idx[None, :])]).astype(BF16)
    lanes = jnp.arange(DN_QK) // DN_DK
    gate_rows = jnp.arange(GATE_PAD)
    esel = jnp.stack([(gate_rows[:, None] == (off + lanes)[None, :]).astype(BF16)
                      for off in (0, DN_HEADS, 2 * DN_HEADS, 3 * DN_HEADS)])
    gi = jnp.arange(GROUP_W) // DN_DK
    bd = (gi[:, None] == gi[None, :]).astype(BF16)
    hi = jnp.arange(DN_QK) // DN_DK
    ones_hd = (hi[:, None] == hi[None, :]).astype(BF16)
    return {"tri": tri, "esel": esel, "bd": bd, "ones_hd": ones_hd}


def _pick(total, want):
    t = min(want, total)
    while total % t:
        t //= 2
    return t


def kernel(x, w_in, conv_w, a_log, dt_bias, dn_norm_w, attn_sink, w_up_a, w_up_b, w_out,
           norm_mix_pre, norm_mix_post, norm_mlp_pre, norm_mlp_post, w_mlp_in, w_mlp_out):
    b, s, d = x.shape
    depth = w_in.shape[0]
    m = b * s
    tm = _pick(s, 512)
    ts_delta = _pick(s, 256)
    tq = _pick(s, 512)
    consts = _constants(ts_delta)

    o_dz = DN_CONV_CH
    o_g = o_dz + DN_V
    o_aq = o_g + N_GATES
    o_ak = o_aq + ATTN_Q
    o_bg = o_ak + 2 * ATTN_KV
    cols = {"bg": 0, "z": 2 * d, "aq": 2 * d + DN_V, "akv": 2 * d + DN_V + ATTN_Q}
    head_order = [h for g in range(ATTN_GROUP) for h in (g, g + ATTN_GROUP)]
    rows = lambda v: v.reshape(depth, 1, -1).astype(F32)

    aq_cols = [w_in[:, :, o_aq + h * ATTN_HD:o_aq + (h + 1) * ATTN_HD] * (ATTN_HD ** -0.5)
               for h in head_order]
    w_main = jnp.concatenate(
        [w_in[:, :, o_bg:o_bg + 2 * d], w_in[:, :, o_dz:o_g]] + aq_cols
        + [w_in[:, :, o_ak:o_ak + 2 * ATTN_KV]], axis=2).astype(BF16)
    w_qkv = w_in[:, :, 0:DN_CONV_CH].astype(BF16)
    w_gate = jnp.pad(w_in[:, :, o_g:o_g + N_GATES],
                     ((0, 0), (0, 0), (0, GATE_PAD - N_GATES))).astype(BF16)
    wb_rows = jnp.concatenate(
        [w_up_b[:, h * ATTN_HD:(h + 1) * ATTN_HD] for h in head_order], axis=1).astype(BF16)
    wa, wo = w_up_a.astype(BF16), w_out.astype(BF16)
    w1, w2 = w_mlp_in.astype(BF16), w_mlp_out.astype(BF16)
    cw = conv_w.astype(F32)
    gate_pad = ((0, 0), (0, 0), (0, GATE_PAD - 2 * DN_HEADS))
    alog_rows = jnp.pad(a_log.reshape(depth, 1, -1).astype(F32), gate_pad)
    dtb_rows = jnp.pad(dt_bias.reshape(depth, 1, -1).astype(F32), gate_pad)
    sink_b = jnp.broadcast_to(attn_sink.astype(F32)[:, :, None], (depth, ATTN_HEADS, LANES))
    dnw = jnp.tile(dn_norm_w.astype(F32), (1, DN_HEADS)).reshape(depth, 1, DN_V)
    n_mix_pre, n_mix_post = rows(norm_mix_pre), rows(norm_mix_post)
    n_mlp_pre, n_mlp_post = rows(norm_mlp_pre), rows(norm_mlp_post)

    xf = x.reshape(m, d)
    for l in range(depth):
        qkvn, p, gates = _in_proj(xf, n_mix_pre, w_qkv, w_main, w_gate, cw, consts["ones_hd"], tm, s, l)
        o_f, o_b = _delta(qkvn.reshape(b, s, DN_CONV_CH), gates.reshape(b, s, GATE_PAD),
                          alog_rows, dtb_rows, consts, ts_delta, l)
        o_att = _attn(p.reshape(b, s, p.shape[1]), sink_b, cols["aq"], cols["akv"], tq, l)
        xf = _mix_mlp(o_f.reshape(m, DN_V), o_b.reshape(m, DN_V), p, o_att.reshape(m, ATTN_Q), xf,
                      dnw, consts["ones_hd"], wa, wb_rows, wo, n_mix_post, n_mlp_pre, w1, w2,
                      n_mlp_post, cols, tm, l)
    return xf.reshape(b, s, d)
```

```python
import functools

import jax
import jax.numpy as jnp
from jax import lax
from jax.experimental import pallas as pl
from jax.experimental.pallas import tpu as pltpu

F32 = jnp.float32
BF16 = jnp.bfloat16

NORM_EPS = 1e-6
DN_HEADS = 8
DN_DK = 64
DN_DV = 64
DN_CONV = 5
DN_CHUNK = 64
ATTN_HEADS = 8
ATTN_KV_HEADS = 2
ATTN_GROUP = ATTN_HEADS // ATTN_KV_HEADS
ATTN_HD = 64
WINDOW = 128
ATTN_BLOCK = 128

DN_QK = DN_HEADS * DN_DK
DN_V = DN_HEADS * DN_DV
DN_CONV_CH = 2 * DN_QK + DN_V
ATTN_Q = ATTN_HEADS * ATTN_HD
ATTN_KV = ATTN_KV_HEADS * ATTN_HD
N_GATES = 4 * DN_HEADS

LANES = 128
GATE_PAD = LANES
HEADS_PER_GROUP = 4
GROUP_W = HEADS_PER_GROUP * DN_DK
N_GROUPS = DN_HEADS // HEADS_PER_GROUP

VMEM_LIMIT = 56 * 1024 * 1024


def _cparams(sem):
    return pltpu.CompilerParams(dimension_semantics=sem, vmem_limit_bytes=VMEM_LIMIT)


def _const_spec(shape):
    nd = len(shape)
    return pl.BlockSpec(shape, lambda *_: (0,) * nd)


def _layer_spec(shape, layer, single=False):
    nd = len(shape)
    kw = {"pipeline_mode": pl.Buffered(1)} if single else {}
    return pl.BlockSpec((None,) + tuple(shape), lambda *_: (layer,) + (0,) * nd, **kw)


def _dot(a, b):
    return jnp.dot(a, b, preferred_element_type=F32)


def _dot_nt(a, b):
    return lax.dot_general(a, b, (((1,), (1,)), ((), ())), preferred_element_type=F32)


def _dot_tn(a, b):
    return lax.dot_general(a, b, (((0,), (0,)), ((), ())), preferred_element_type=F32)


def _split_terms(x, terms):
    out, r = [], x
    for t in range(terms):
        hi = r.astype(BF16)
        out.append(hi)
        if t + 1 < terms:
            r = r - hi.astype(F32)
    return out


def _sel_dot(x, sel, terms):
    acc = None
    for piece in _split_terms(x, terms):
        d = _dot(piece, sel)
        acc = d if acc is None else acc + d
    return acc


def _sel_dot_left(sel, x, terms):
    acc = None
    for piece in _split_terms(x, terms):
        d = _dot(sel, piece)
        acc = d if acc is None else acc + d
    return acc


def _rms(x, w):
    ms = jnp.mean(x * x, axis=-1, keepdims=True)
    return x * lax.rsqrt(ms + NORM_EPS) * w


X_HALO = 8


def _in_proj_kernel(x_ref, xp_ref, xn_ref, nw_ref, wqkv_ref, w_ref, wg_ref, cw_ref, ones_ref,
                    qkv_ref, p_ref, g_ref, ext_ref, *, tiles_per_seq):
    jj = pl.program_id(0) % tiles_per_seq
    tm = x_ref.shape[0]
    half = tm // 2
    pad = DN_CONV // 2
    nw = nw_ref[...]
    h_a = _rms(jnp.concatenate([xp_ref[...], x_ref[0:half, :]], axis=0), nw)
    pre_a = _dot(h_a.astype(BF16), wqkv_ref[...])
    h_b = _rms(jnp.concatenate([x_ref[half:, :], xn_ref[...]], axis=0), nw)
    pre_b = _dot(h_b.astype(BF16), wqkv_ref[...])
    ext_ref[0:X_HALO, :] = jnp.where(jj > 0, pre_a[0:X_HALO], 0.0)
    ext_ref[X_HALO:X_HALO + half, :] = pre_a[X_HALO:]
    ext_ref[X_HALO + half:X_HALO + tm, :] = pre_b[0:half]
    ext_ref[X_HALO + tm:, :] = jnp.where(jj < tiles_per_seq - 1, pre_b[half:], 0.0)

    h = jnp.concatenate([h_a[X_HALO:], h_b[0:half]], axis=0).astype(BF16)
    ones = ones_ref[...]

    def conv_silu(c0, c1):
        acc = None
        for k in range(DN_CONV):
            off = X_HALO - pad + k
            term = ext_ref[off:off + tm, c0:c1] * cw_ref[k:k + 1, c0:c1]
            acc = term if acc is None else acc + term
        return acc * jax.nn.sigmoid(acc)

    def l2n(t, scale):
        ss = _sel_dot(t * t, ones, 1)
        return t * (lax.rsqrt(ss + NORM_EPS) * scale)

    def emit_q():
        qkv_ref[:, 0:DN_QK] = l2n(conv_silu(0, DN_QK), DN_DK ** -0.5).astype(BF16)

    def emit_k():
        qkv_ref[:, DN_QK:2 * DN_QK] = l2n(conv_silu(DN_QK, 2 * DN_QK), 1.0).astype(BF16)

    def emit_v():
        qkv_ref[:, 2 * DN_QK:] = conv_silu(2 * DN_QK, DN_CONV_CH).astype(BF16)

    epilogue = [emit_q, emit_k, emit_v]
    n = w_ref.shape[1]
    step = 1024
    for idx, c0 in enumerate(range(0, n, step)):
        c1 = min(c0 + step, n)
        p_ref[:, c0:c1] = _dot(h, w_ref[:, c0:c1]).astype(BF16)
        if idx < len(epilogue):
            epilogue[idx]()
    g_ref[...] = _dot(h, wg_ref[...])


def _in_proj(xf, nw, wqkv, w, wg, cw, ones_hd, tm, s, layer):
    m, d = xf.shape
    n = w.shape[2]
    nh = tm // X_HALO
    last = m // X_HALO - 1
    return pl.pallas_call(
        functools.partial(_in_proj_kernel, tiles_per_seq=s // tm),
        grid=(m // tm,),
        in_specs=[
            pl.BlockSpec((tm, d), lambda i: (i, 0)),
            pl.BlockSpec((X_HALO, d), lambda i: (jnp.maximum(i * nh - 1, 0), 0)),
            pl.BlockSpec((X_HALO, d), lambda i: (jnp.minimum((i + 1) * nh, last), 0)),
            _layer_spec((1, d), layer),
            _layer_spec((d, DN_CONV_CH), layer, single=True),
            _layer_spec((d, n), layer, single=True),
            _layer_spec((d, GATE_PAD), layer),
            _layer_spec((DN_CONV, DN_CONV_CH), layer),
            _const_spec((DN_QK, DN_QK)),
        ],
        out_specs=[
            pl.BlockSpec((tm, DN_CONV_CH), lambda i: (i, 0)),
            pl.BlockSpec((tm, n), lambda i: (i, 0)),
            pl.BlockSpec((tm, GATE_PAD), lambda i: (i, 0)),
        ],
        out_shape=[
            jax.ShapeDtypeStruct((m, DN_CONV_CH), BF16),
            jax.ShapeDtypeStruct((m, n), BF16),
            jax.ShapeDtypeStruct((m, GATE_PAD), F32),
        ],
        scratch_shapes=[pltpu.VMEM((tm + 2 * X_HALO, DN_CONV_CH), F32)],
        compiler_params=_cparams(("parallel",)),
        name="in_proj",
    )(xf, xf, xf, nw, wqkv, w, wg, cw, ones_hd)


def _softplus(x):
    return jnp.maximum(x, 0.0) + jnp.log1p(jnp.exp(-jnp.abs(x)))


def _block_diag(x, half_masks):
    xb = x.astype(BF16)
    zeros = jnp.zeros((xb.shape[0], LANES), BF16)
    blocks = []
    for h in range(HEADS_PER_GROUP):
        tile, half = divmod(h, 2)
        kept = xb[:, tile * LANES:(tile + 1) * LANES] * half_masks[half]
        blocks.append(jnp.concatenate([kept, zeros] if tile == 0 else [zeros, kept], axis=1))
    return jnp.concatenate(blocks, axis=0)


def _mask_block_diag(bd, tile_mask):
    c = tile_mask.shape[0]
    zeros = jnp.zeros((c, LANES), BF16)
    blocks = []
    for h in range(HEADS_PER_GROUP):
        tile = h // 2
        kept = bd[h * c:(h + 1) * c, tile * LANES:(tile + 1) * LANES] * tile_mask
        blocks.append(jnp.concatenate([kept, zeros] if tile == 0 else [zeros, kept], axis=1))
    return jnp.concatenate(blocks, axis=0)


def _round_robin(gens):
    live = list(gens)
    while live:
        nxt = []
        for g in live:
            try:
                next(g)
                nxt.append(g)
            except StopIteration:
                pass
        live = nxt


def _delta_kernel(qkvf_ref, qkvb_ref, gf_ref, gb_ref, alog_ref, dtb_ref,
                  tri_ref, esel_ref, bd_ref,
                  of_ref, ob_ref,
                  gc_ref, beta_ref, u_ref, wq_ref, attn_ref, kg_ref, state_ref):
    j = pl.program_id(1)
    bb, ts = qkvf_ref.shape[0], qkvf_ref.shape[1]
    n_chunks = ts // DN_CHUNK
    c = DN_CHUNK

    @pl.when(j == 0)
    def _():
        state_ref[...] = jnp.zeros_like(state_ref)

    lane_g = lax.broadcasted_iota(jnp.int32, (1, GATE_PAD), 1)
    rate = jnp.where(lane_g < 2 * DN_HEADS, jnp.exp(alog_ref[...]), 0.0)
    dtb = dtb_ref[...]

    for bi in range(bb):
        for d, g_ref in ((0, gf_ref), (1, gb_ref)):
            gates = g_ref[bi]
            g = -rate * _softplus(gates + dtb)
            gcd = _sel_dot_left(tri_ref[d], g, 2)
            gc_ref[2 * bi + d] = _sel_dot(gcd, esel_ref[d], 2)
            beta_ref[2 * bi + d] = _sel_dot(jax.nn.sigmoid(gates), esel_ref[2 + d], 1)

    bd_f32 = bd_ref[...].astype(F32)
    lane_t = lax.broadcasted_iota(jnp.int32, (c, LANES), 1)
    half_masks = ((lane_t < DN_DK).astype(BF16), (lane_t >= DN_DK).astype(BF16))
    row = lax.broadcasted_iota(jnp.int32, (c, GROUP_W), 0)
    col = lax.broadcasted_iota(jnp.int32, (c, GROUP_W), 1) % c
    eye = row == col
    eye_f = eye.astype(F32)
    masks = ((row >= col, row > col), (row <= col, row < col))
    base = 8
    same_blk = lambda bs: (row // bs) == (col // bs)
    base_blk = same_blk(base)
    merge_masks = []
    bs = base
    while bs < c:
        off_blk = jnp.logical_and(same_blk(2 * bs), jnp.logical_not(same_blk(bs)))
        merge_masks.append(off_blk[:, :LANES].astype(BF16))
        bs *= 2

    def slot_of(bi, d, hg, chunk):
        return ((2 * bi + d) * N_GROUPS + hg) * n_chunks + chunk

    def prep(qkv_ref, bi, chunk, d, hg):
        rows = pl.ds(pl.multiple_of(chunk * c, c), c)
        l0 = hg * GROUP_W
        q = qkv_ref[bi, rows, l0:l0 + GROUP_W]
        k = qkv_ref[bi, rows, DN_QK + l0:DN_QK + l0 + GROUP_W]
        v = qkv_ref[bi, rows, 2 * DN_QK + l0:2 * DN_QK + l0 + GROUP_W]
        incl, strict = masks[d]
        slot = slot_of(bi, d, hg, chunk)

        bdk = _block_diag(k, half_masks)
        r = _dot_nt(jnp.concatenate([q, k], axis=0), bdk)
        yield
        gcc = gc_ref[2 * bi + d, rows, l0:l0 + GROUP_W]
        beta = beta_ref[2 * bi + d, rows, l0:l0 + GROUP_W]
        gcrow = jnp.sum(jnp.where(eye, gcc, 0.0), axis=0, keepdims=True)
        decay = jnp.where(incl, jnp.exp(jnp.where(incl, gcc - gcrow, 0.0)), 0.0)
        attn_ref[slot] = (r[:c] * decay).astype(BF16)
        nmat = jnp.where(strict, -(r[c:] * beta * decay), 0.0)

        n_base = jnp.where(base_blk, nmat, 0.0)
        p = eye_f + n_base
        n2 = _dot(n_base.astype(BF16), _block_diag(n_base, half_masks))
        yield
        r = _dot(jnp.concatenate([p, n2], axis=0).astype(BF16), _block_diag(n2, half_masks))
        yield
        p = p + r[:c]
        x = _dot(p.astype(BF16), _block_diag(r[c:], half_masks))
        yield
        p = p + x
        bdn = _block_diag(nmat, half_masks)
        for tile_mask in merge_masks:
            pb = p.astype(BF16)
            t = _dot(pb, _mask_block_diag(bdn, tile_mask))
            yield
            x = _dot(t.astype(BF16), _block_diag(pb, half_masks))
            yield
            p = p + x

        kf, vf = k.astype(F32), v.astype(F32)
        egc = jnp.exp(gcc)
        g_end = gcc[c - 1:c, :] if d == 0 else gcc[0:1, :]
        rhs = jnp.concatenate([_block_diag(vf * beta, half_masks),
                               _block_diag(kf * (beta * egc), half_masks)], axis=1)
        uw = _dot(p.astype(BF16), rhs)
        yield
        u_ref[slot] = uw[:, :GROUP_W]
        wq_ref[slot, 0:c, :] = uw[:, GROUP_W:].astype(BF16)
        wq_ref[slot, c:2 * c, :] = (q.astype(F32) * egc).astype(BF16)
        kg_ref[slot] = (kf * jnp.exp(g_end - gcc)).astype(BF16)

    def scan(o_ref, bi, chunk, d, hg):
        r0 = pl.multiple_of(chunk * c, c)
        l0 = hg * GROUP_W
        slot = slot_of(bi, d, hg, chunk)
        sidx = (2 * bi + d) * N_GROUPS + hg
        state = state_ref[sidx]
        r2 = _dot(wq_ref[slot], state.astype(BF16))
        yield
        vnew = u_ref[slot] - r2[:c]
        o = r2[c:] + _dot(attn_ref[slot], _block_diag(vnew, half_masks))
        upd = _dot_tn(kg_ref[slot], vnew.astype(BF16))
        yield
        o_ref[bi, pl.ds(r0, c), l0:l0 + GROUP_W] = o.astype(o_ref.dtype)
        end_row = r0 + (c - 1 if d == 0 else 0)
        etot = jnp.exp(gc_ref[2 * bi + d, pl.ds(end_row, 1), l0:l0 + GROUP_W])
        state_ref[sidx] = state * etot + upd * bd_f32

    def prep_body(bi, carry):
        gens = []
        for chunk in range(n_chunks):
            for hg in range(N_GROUPS):
                gens.append(prep(qkvf_ref, bi, chunk, 0, hg))
                gens.append(prep(qkvb_ref, bi, chunk, 1, hg))
        _round_robin(gens)
        return carry

    lax.fori_loop(0, bb, prep_body, 0)

    def scan_body(ci, carry):
        gens = []
        for bi in range(bb):
            for hg in range(N_GROUPS):
                gens.append(scan(of_ref, bi, ci, 0, hg))
                gens.append(scan(ob_ref, bi, n_chunks - 1 - ci, 1, hg))
        _round_robin(gens)
        return carry

    lax.fori_loop(0, n_chunks, scan_body, 0)


def _delta(qkvn, gates3, alog_row, dtb_row, consts, ts, layer):
    b, s, _ = qkvn.shape
    nj = s // ts
    bb = 4 if b % 4 == 0 else (2 if b % 2 == 0 else 1)
    n_slots = bb * 2 * N_GROUPS * (ts // DN_CHUNK)
    fwd = lambda i, j: (i, j, 0)
    bwd = lambda i, j: (i, nj - 1 - j, 0)
    return pl.pallas_call(
        _delta_kernel,
        grid=(b // bb, nj),
        in_specs=[
            pl.BlockSpec((bb, ts, DN_CONV_CH), fwd),
            pl.BlockSpec((bb, ts, DN_CONV_CH), bwd),
            pl.BlockSpec((bb, ts, GATE_PAD), fwd),
            pl.BlockSpec((bb, ts, GATE_PAD), bwd),
            _layer_spec((1, GATE_PAD), layer),
            _layer_spec((1, GATE_PAD), layer),
            _const_spec((2, ts, ts)),
            _const_spec((4, GATE_PAD, DN_QK)),
            _const_spec((GROUP_W, GROUP_W)),
        ],
        out_specs=[
            pl.BlockSpec((bb, ts, DN_V), fwd),
            pl.BlockSpec((bb, ts, DN_V), bwd),
        ],
        out_shape=[
            jax.ShapeDtypeStruct((b, s, DN_V), BF16),
            jax.ShapeDtypeStruct((b, s, DN_V), BF16),
        ],
        scratch_shapes=[
            pltpu.VMEM((2 * bb, ts, DN_QK), F32),
            pltpu.VMEM((2 * bb, ts, DN_QK), F32),
            pltpu.VMEM((n_slots, DN_CHUNK, GROUP_W), F32),
            pltpu.VMEM((n_slots, 2 * DN_CHUNK, GROUP_W), BF16),
            pltpu.VMEM((n_slots, DN_CHUNK, GROUP_W), BF16),
            pltpu.VMEM((n_slots, DN_CHUNK, GROUP_W), BF16),
            pltpu.VMEM((2 * bb * N_GROUPS, GROUP_W, GROUP_W), F32),
        ],
        compiler_params=_cparams(("parallel", "arbitrary")),
        name="gated_delta",
    )(qkvn, qkvn, gates3, gates3, alog_row, dtb_row,
      consts["tri"], consts["esel"], consts["bd"])


def _attn_kernel(q_ref, kvc_ref, kvp_ref, kvn_ref, sink_ref, o_ref, kv_ref, bias_ref):
    j = pl.program_id(1)
    nj = pl.num_programs(1)
    tq = q_ref.shape[1]
    blk = ATTN_BLOCK
    span = blk + 2 * WINDOW
    n_sub = tq // blk
    n_heads = 2 * ATTN_GROUP

    def head_of(idx):
        return idx // 2 + ATTN_GROUP * (idx % 2)

    @pl.when(j == 0)
    def _():
        qi = lax.broadcasted_iota(jnp.int32, (blk, span), 0)
        kr = lax.broadcasted_iota(jnp.int32, (blk, span), 1)
        dist_i = jnp.abs(qi + WINDOW - kr)
        dist = dist_i.astype(F32)
        for idx in range(n_heads):
            slope = 2.0 ** (-8.0 * (head_of(idx) + 1) / ATTN_HEADS)
            bias_ref[idx] = jnp.where(dist_i <= WINDOW, -slope * dist, -jnp.inf)

    kv_ref[0:WINDOW] = kvp_ref[0]
    kv_ref[WINDOW:WINDOW + tq] = kvc_ref[0]
    kv_ref[WINDOW + tq:] = kvn_ref[0]

    lane = lax.broadcasted_iota(jnp.int32, (1, LANES), 1)
    lo = lane < ATTN_HD
    kcol = lax.broadcasted_iota(jnp.int32, (1, span), 1)
    first_ok = jnp.logical_or(j > 0, kcol >= WINDOW)
    last_ok = jnp.logical_or(j < nj - 1, kcol < WINDOW + blk)
    zero = jnp.zeros((), BF16)
    orow = lax.broadcasted_iota(jnp.int32, (2 * span, LANES), 0)
    olane = lax.broadcasted_iota(jnp.int32, (2 * span, LANES), 1)
    ones_sel = ((orow < span) == (olane < ATTN_HD)).astype(BF16)

    def scores(sb):
        kwin = kv_ref[sb * blk:sb * blk + span, 0:ATTN_KV]
        qs = []
        for g in range(ATTN_GROUP):
            qg = q_ref[0, sb * blk:(sb + 1) * blk, g * LANES:(g + 1) * LANES]
            qs.append(jnp.where(lo, qg, zero))
            qs.append(jnp.where(lo, zero, qg))
        return _dot_nt(jnp.concatenate(qs, axis=0), kwin)

    def finish(sb, s_all):
        vwin = kv_ref[sb * blk:sb * blk + span, ATTN_KV:2 * ATTN_KV]
        es, sink_terms = [], []
        for idx in range(n_heads):
            head = head_of(idx)
            sink = sink_ref[head:head + 1, 0:1]
            sc = s_all[idx * blk:(idx + 1) * blk] + bias_ref[idx]
            if sb == 0:
                sc = jnp.where(first_ok, sc, -jnp.inf)
            if sb == n_sub - 1:
                sc = jnp.where(last_ok, sc, -jnp.inf)
            tiles = [sc[:, t * LANES:(t + 1) * LANES] for t in range(span // LANES)]
            m = jnp.max(functools.reduce(jnp.maximum, tiles), axis=-1, keepdims=True)
            m_b = jnp.broadcast_to(jnp.maximum(m, sink), (blk, LANES))
            es.append(jnp.concatenate([jnp.exp(t - m_b) for t in tiles], axis=1).astype(BF16))
            sink_terms.append(jnp.exp(sink - m_b))
        e_all = jnp.concatenate(
            [jnp.concatenate([es[2 * g], es[2 * g + 1]], axis=1) for g in range(ATTN_GROUP)],
            axis=0)
        v_stack = jnp.concatenate([jnp.where(lo, vwin, zero), jnp.where(lo, zero, vwin)], axis=0)
        nd = _dot(e_all, jnp.concatenate([v_stack, ones_sel], axis=1))
        for g in range(ATTN_GROUP):
            rows = slice(g * blk, (g + 1) * blk)
            den = nd[rows, LANES:] + jnp.where(lo, sink_terms[2 * g], sink_terms[2 * g + 1])
            o_ref[0, sb * blk:(sb + 1) * blk, g * LANES:(g + 1) * LANES] = (
                nd[rows, :LANES] / den).astype(BF16)

    for sb in range(n_sub):
        finish(sb, scores(sb))


def _attn(p3, sink_b, q_col, kv_col, tq, layer):
    b, s, _ = p3.shape
    nh = tq // WINDOW
    last = s // WINDOW - 1
    kvw = 2 * ATTN_KV
    span = ATTN_BLOCK + 2 * WINDOW
    return pl.pallas_call(
        _attn_kernel,
        grid=(b, s // tq),
        in_specs=[
            pl.BlockSpec((1, tq, ATTN_Q), lambda i, j: (i, j, q_col // ATTN_Q)),
            pl.BlockSpec((1, tq, kvw), lambda i, j: (i, j, kv_col // kvw)),
            pl.BlockSpec((1, WINDOW, kvw), lambda i, j: (i, jnp.maximum(j * nh - 1, 0), kv_col // kvw)),
            pl.BlockSpec((1, WINDOW, kvw), lambda i, j: (i, jnp.minimum((j + 1) * nh, last), kv_col // kvw)),
            _layer_spec((ATTN_HEADS, LANES), layer),
        ],
        out_specs=pl.BlockSpec((1, tq, ATTN_Q), lambda i, j: (i, j, 0)),
        out_shape=jax.ShapeDtypeStruct((b, s, ATTN_Q), BF16),
        scratch_shapes=[
            pltpu.VMEM((tq + 2 * WINDOW, kvw), BF16),
            pltpu.VMEM((ATTN_HEADS, ATTN_BLOCK, span), F32),
        ],
        compiler_params=_cparams(("parallel", "arbitrary")),
        name="window_attn",
    )(p3, p3, p3, p3, sink_b)


def _mix_mlp_kernel(of_ref, ob_ref, z_ref, oatt_ref, bg_ref, x_ref,
                    dnw_ref, ones_ref, wa_ref, wb_ref, wo_ref, nmix_ref,
                    npre_ref, w1_ref, w2_ref, npost_ref, out_ref):
    d_model = x_ref.shape[1]
    oa = of_ref[...].astype(F32) + ob_ref[...].astype(F32)
    ms = _sel_dot(oa * oa, ones_ref[...], 1) * (1.0 / DN_DV)
    z = z_ref[...].astype(F32)
    on = oa * lax.rsqrt(ms + NORM_EPS) * dnw_ref[...] * (z * jax.nn.sigmoid(z))
    ya = _dot(on.astype(BF16), wa_ref[...])
    yb = _dot(oatt_ref[...], wb_ref[...])
    ga = jax.nn.sigmoid(bg_ref[:, 0:d_model].astype(F32))
    gb = jax.nn.sigmoid(bg_ref[:, d_model:2 * d_model].astype(F32))
    mix = _dot((ga * ya + gb * yb).astype(BF16), wo_ref[...])
    x1 = x_ref[...] + _rms(mix, nmix_ref[...])

    h = _rms(x1, npre_ref[...]).astype(BF16)
    d_ff = w1_ref.shape[1]
    step = 1024
    acc = None
    for c0 in range(0, d_ff, step):
        u = jnp.maximum(_dot(h, w1_ref[:, c0:c0 + step]), 0.0)
        t = _dot((u * u).astype(BF16), w2_ref[c0:c0 + step, :])
        acc = t if acc is None else acc + t
    out_ref[...] = x1 + _rms(acc, npost_ref[...])


def _mix_mlp(o_f, o_b, p, o_att, xf, dnw, ones_hd, wa, wb, wo, nmix, npre, w1, w2, npost, cols, tm,
             layer):
    m, d = xf.shape
    d_ff = w1.shape[2]
    row = lambda i: (i, 0)
    resident = lambda shape: _layer_spec(shape, layer, single=True)
    vec = lambda width: _layer_spec((1, width), layer)
    return pl.pallas_call(
        _mix_mlp_kernel,
        grid=(m // tm,),
        in_specs=[
            pl.BlockSpec((tm, DN_V), row),
            pl.BlockSpec((tm, DN_V), row),
            pl.BlockSpec((tm, DN_V), lambda i: (i, cols["z"] // DN_V)),
            pl.BlockSpec((tm, ATTN_Q), row),
            pl.BlockSpec((tm, 2 * d), lambda i: (i, cols["bg"] // (2 * d))),
            pl.BlockSpec((tm, d), row),
            vec(DN_V),
            _const_spec((DN_V, DN_V)),
            resident((DN_V, d)),
            resident((ATTN_Q, d)),
            resident((d, d)),
            vec(d),
            vec(d),
            resident((d, d_ff)),
            resident((d_ff, d)),
            vec(d),
        ],
        out_specs=pl.BlockSpec((tm, d), row),
        out_shape=jax.ShapeDtypeStruct((m, d), F32),
        compiler_params=_cparams(("parallel",)),
        name="mix_mlp",
    )(o_f, o_b, p, o_att, p, xf, dnw, ones_hd, wa, wb, wo, nmix, npre, w1, w2, npost)


def _constants(ts):
    idx = jnp.arange(ts)
    same_chunk = (idx[:, None] // DN_CHUNK) == (idx[None, :] // DN_CHUNK)
    tri = jnp.stack([jnp.logical_and(same_chunk, idx[:, None] >= idx[None, :]),
                     jnp.logical_and(same_chunk, idx[:, None] <= idx[None, :])]).astype(BF16)
    lanes = jnp.arange(DN_QK) // DN_DK
    gate_rows = jnp.arange(GATE_PAD)
    esel = jnp.stack([(gate_rows[:, None] == (off + lanes)[None, :]).astype(BF16)
                      for off in (0, DN_HEADS, 2 * DN_HEADS, 3 * DN_HEADS)])
    gi = jnp.arange(GROUP_W) // DN_DK
    bd = (gi[:, None] == gi[None, :]).astype(BF16)
    hi = jnp.arange(DN_QK) // DN_DK
    ones_hd = (hi[:, None] == hi[None, :]).astype(BF16)
    return {"tri": tri, "esel": esel, "bd": bd, "ones_hd": ones_hd}


def _pick(total, want):
    t = min(want, total)
    while total % t:
        t //= 2
    return t


def kernel(x, w_in, conv_w, a_log, dt_bias, dn_norm_w, attn_sink, w_up_a, w_up_b, w_out,
           norm_mix_pre, norm_mix_post, norm_mlp_pre, norm_mlp_post, w_mlp_in, w_mlp_out):
    b, s, d = x.shape
    depth = w_in.shape[0]
    m = b * s
    tm = _pick(s, 512)
    ts_delta = _pick(s, 256)
    tq = _pick(s, 1024)
    consts = _constants(ts_delta)

    o_dz = DN_CONV_CH
    o_g = o_dz + DN_V
    o_aq = o_g + N_GATES
    o_ak = o_aq + ATTN_Q
    o_bg = o_ak + 2 * ATTN_KV
    cols = {"bg": 0, "z": 2 * d, "aq": 2 * d + DN_V, "akv": 2 * d + DN_V + ATTN_Q}
    head_order = [h for g in range(ATTN_GROUP) for h in (g, g + ATTN_GROUP)]
    rows = lambda v: v.reshape(depth, 1, -1).astype(F32)

    aq_cols = [w_in[:, :, o_aq + h * ATTN_HD:o_aq + (h + 1) * ATTN_HD] * (ATTN_HD ** -0.5)
               for h in head_order]
    w_main = jnp.concatenate(
        [w_in[:, :, o_bg:o_bg + 2 * d], w_in[:, :, o_dz:o_g]] + aq_cols
        + [w_in[:, :, o_ak:o_ak + 2 * ATTN_KV]], axis=2).astype(BF16)
    w_qkv = w_in[:, :, 0:DN_CONV_CH].astype(BF16)
    w_gate = jnp.pad(w_in[:, :, o_g:o_g + N_GATES],
                     ((0, 0), (0, 0), (0, GATE_PAD - N_GATES))).astype(BF16)
    wb_rows = jnp.concatenate(
        [w_up_b[:, h * ATTN_HD:(h + 1) * ATTN_HD] for h in head_order], axis=1).astype(BF16)
    wa, wo = w_up_a.astype(BF16), w_out.astype(BF16)
    w1, w2 = w_mlp_in.astype(BF16), w_mlp_out.astype(BF16)
    cw = conv_w.astype(F32)
    gate_pad = ((0, 0), (0, 0), (0, GATE_PAD - 2 * DN_HEADS))
    alog_rows = jnp.pad(a_log.reshape(depth, 1, -1).astype(F32), gate_pad)
    dtb_rows = jnp.pad(dt_bias.reshape(depth, 1, -1).astype(F32), gate_pad)
    sink_b = jnp.broadcast_to(attn_sink.astype(F32)[:, :, None], (depth, ATTN_HEADS, LANES))
    dnw = jnp.tile(dn_norm_w.astype(F32), (1, DN_HEADS)).reshape(depth, 1, DN_V)
    n_mix_pre, n_mix_post = rows(norm_mix_pre), rows(norm_mix_post)
    n_mlp_pre, n_mlp_post = rows(norm_mlp_pre), rows(norm_mlp_post)

    xf = x.reshape(m, d)
    for l in range(depth):
        qkvn, p, gates = _in_proj(xf, n_mix_pre, w_qkv, w_main, w_gate, cw, consts["ones_hd"], tm, s, l)
        o_f, o_b = _delta(qkvn.reshape(b, s, DN_CONV_CH), gates.reshape(b, s, GATE_PAD),
                          alog_rows, dtb_rows, consts, ts_delta, l)
        o_att = _attn(p.reshape(b, s, p.shape[1]), sink_b, cols["aq"], cols["akv"], tq, l)
        xf = _mix_mlp(o_f.reshape(m, DN_V), o_b.reshape(m, DN_V), p, o_att.reshape(m, ATTN_Q), xf,
                      dnw, consts["ones_hd"], wa, wb_rows, wo, n_mix_post, n_mlp_pre, w1, w2,
                      n_mlp_post, cols, tm, l)
    return xf.reshape(b, s, d)
```
